```python
import math
import jax, jax.numpy as jnp
from jax import lax
import numpy as np

D_MODEL = 1024
BATCH = 8
SEQ = 4096
DEPTH = 1

RET_HEADS = 8
RET_DK = 128
RET_DV = 256
RET_CHUNK = 128
ROPE_BASE = 10000.0
CONV_CH = D_MODEL
CONV_WIDTH = 31
D_FF = 2816
LN_EPS = 1e-5

QK_W = RET_HEADS * RET_DK
V_W = RET_HEADS * RET_DV
SPLIT_POINTS = tuple(int(p) for p in np.cumsum([QK_W, QK_W, V_W, V_W, CONV_CH, CONV_CH, D_MODEL]))
IN_W = 2 * QK_W + 2 * V_W + 2 * CONV_CH + 2 * D_MODEL

kernel_name = "hybrid_retention_conformer_macaron_deepnorm"


def layer_norm(x, g, b):
    xf = x.astype(jnp.float32)
    mu = jnp.mean(xf, axis=-1, keepdims=True)
    var = jnp.mean(jnp.square(xf - mu), axis=-1, keepdims=True)
    y = (xf - mu) * lax.rsqrt(var + LN_EPS) * g.astype(jnp.float32) + b.astype(jnp.float32)
    return y.astype(x.dtype)


def swiglu_ffn(h, w_gate, w_up, w_down):
    return (jax.nn.silu(h @ w_gate) * (h @ w_up)) @ w_down


def rotary(x, cos, sin):
    half = x.shape[-1] // 2
    x1, x2 = x[..., :half], x[..., half:]
    return jnp.concatenate([x1 * cos - x2 * sin, x2 * cos + x1 * sin], axis=-1).astype(x.dtype)


def retention_chunkwise(q, k, v):
    b, s, h, dk = q.shape
    dv = v.shape[-1]
    c = RET_CHUNK
    n = s // c
    log_g = jnp.log(1.0 - jnp.exp2(-5.0 - jnp.arange(h, dtype=jnp.float32)))
    idx = jnp.arange(c, dtype=jnp.float32)
    diff = idx[:, None] - idx[None, :]
    decay_mask = jnp.where(diff[None] >= 0,
                           jnp.exp(jnp.maximum(diff, 0.0)[None] * log_g[:, None, None]), 0.0)
    qc = q.reshape(b, n, c, h, dk)
    kc = k.reshape(b, n, c, h, dk)
    vc = v.reshape(b, n, c, h, dv)
    scores = jnp.einsum('bnihd,bnjhd->bnhij', qc, kc) * decay_mask
    intra = jnp.einsum('bnhij,bnjhe->bnihe', scores, vc)
    xi = jnp.exp((idx[:, None] + 1.0) * log_g[None, :])
    zeta = jnp.exp((c - 1.0 - idx)[:, None] * log_g[None, :])
    chunk_decay = jnp.exp(c * log_g)

    def step(state, xs):
        qn, kn, vn = xs
        cross = jnp.einsum('bihd,bhde->bihe', qn, state) * xi[None, :, :, None]
        new_state = chunk_decay[None, :, None, None] * state + jnp.einsum(
            'bjhd,bjhe->bhde', kn * zeta[None, :, :, None], vn)
        return new_state, cross

    state0 = jnp.zeros((b, h, dk, dv), jnp.float32)
    _, cross = lax.scan(step, state0, (jnp.moveaxis(qc, 1, 0), jnp.moveaxis(kc, 1, 0), jnp.moveaxis(vc, 1, 0)))
    cross = jnp.moveaxis(cross, 0, 1)
    return (intra + cross).reshape(b, s, h, dv)


def hybrid_mixer(h, w_in, b_in, ret_gn_g, conv_k, conv_b, conv_ln_g, conv_ln_b,
                 w_ret_o, w_conv_o, w_out, cos, sin):
    bsz, s, _ = h.shape
    proj = h @ w_in + b_in
    q, k, v, g, glu_a, glu_b, gate_r, gate_c = jnp.split(proj, SPLIT_POINTS, axis=-1)

    q = rotary(q.reshape(bsz, s, RET_HEADS, RET_DK), cos, sin) * (RET_DK ** -0.5)
    k = rotary(k.reshape(bsz, s, RET_HEADS, RET_DK), cos, sin)
    v = v.reshape(bsz, s, RET_HEADS, RET_DV)
    r = retention_chunkwise(q, k, v)
    mu = jnp.mean(r, axis=-1, keepdims=True)
    var = jnp.mean(jnp.square(r - mu), axis=-1, keepdims=True)
    r = ((r - mu) * lax.rsqrt(var + LN_EPS)).reshape(bsz, s, V_W) * ret_gn_g.astype(jnp.float32)
    ret_out = (jax.nn.silu(g) * r.astype(h.dtype)) @ w_ret_o

    u = glu_a * jax.nn.sigmoid(glu_b)
    u = lax.conv_general_dilated(u, conv_k, window_strides=(1,), padding=[(CONV_WIDTH - 1, 0)],
                                 dimension_numbers=('NWC', 'WIO', 'NWC'),
                                 feature_group_count=CONV_CH) + conv_b
    u = jax.nn.silu(layer_norm(u, conv_ln_g, conv_ln_b))
    conv_out = u @ w_conv_o

    merged = jax.nn.sigmoid(gate_r) * ret_out + jax.nn.sigmoid(gate_c) * conv_out
    return merged @ w_out


def setup_inputs(seed: int = 0) -> dict:
    key = jax.random.key(seed)
    ks = jax.random.split(key, 24)
    beta = (8.0 * DEPTH) ** -0.25
    L = DEPTH

    def nrm(k, shape, scale):
        return jax.random.normal(k, shape, jnp.float32) * scale

    def gain(k, shape):
        return 1.0 + 0.02 * jax.random.normal(k, shape, jnp.float32)

    return {
        "x": jax.random.normal(ks[0], (BATCH, SEQ, D_MODEL), jnp.float32),
        "ffn1_w_gate": nrm(ks[1], (L, D_MODEL, D_FF), D_MODEL ** -0.5),
        "ffn1_w_up": nrm(ks[2], (L, D_MODEL, D_FF), D_MODEL ** -0.5),
        "ffn1_w_down": nrm(ks[3], (L, D_FF, D_MODEL), beta * D_FF ** -0.5),
        "ln1_g": gain(ks[4], (L, D_MODEL)),
        "ln1_b": nrm(ks[5], (L, D_MODEL), 0.02),
        "w_in": nrm(ks[6], (L, D_MODEL, IN_W), D_MODEL ** -0.5),
        "b_in": nrm(ks[7], (L, IN_W), 0.02),
        "ret_gn_g": gain(ks[8], (L, V_W)),
        "conv_k": nrm(ks[9], (L, CONV_WIDTH, 1, CONV_CH), CONV_WIDTH ** -0.5),
        "conv_b": nrm(ks[10], (L, CONV_CH), 0.02),
        "conv_ln_g": gain(ks[11], (L, CONV_CH)),
        "conv_ln_b": nrm(ks[12], (L, CONV_CH), 0.02),
        "w_ret_o": nrm(ks[13], (L, V_W, D_MODEL), beta * V_W ** -0.5),
        "w_conv_o": nrm(ks[14], (L, CONV_CH, D_MODEL), beta * CONV_CH ** -0.5),
        "w_out": nrm(ks[15], (L, D_MODEL, D_MODEL), beta * D_MODEL ** -0.5),
        "ln2_g": gain(ks[16], (L, D_MODEL)),
        "ln2_b": nrm(ks[17], (L, D_MODEL), 0.02),
        "ffn2_w_gate": nrm(ks[18], (L, D_MODEL, D_FF), D_MODEL ** -0.5),
        "ffn2_w_up": nrm(ks[19], (L, D_MODEL, D_FF), D_MODEL ** -0.5),
        "ffn2_w_down": nrm(ks[20], (L, D_FF, D_MODEL), beta * D_FF ** -0.5),
        "ln3_g": gain(ks[21], (L, D_MODEL)),
        "ln3_b": nrm(ks[22], (L, D_MODEL), 0.02),
    }


def reference(x, ffn1_w_gate, ffn1_w_up, ffn1_w_down, ln1_g, ln1_b, w_in, b_in, ret_gn_g,
              conv_k, conv_b, conv_ln_g, conv_ln_b, w_ret_o, w_conv_o, w_out, ln2_g, ln2_b,
              ffn2_w_gate, ffn2_w_up, ffn2_w_down, ln3_g, ln3_b):
    alpha = (2.0 * DEPTH) ** 0.25
    s = x.shape[1]
    half = RET_DK // 2
    freqs = ROPE_BASE ** (-jnp.arange(half, dtype=jnp.float32) / half)
    ang = jnp.arange(s, dtype=jnp.float32)[:, None] * freqs[None, :]
    cos = jnp.cos(ang)[:, None, :]
    sin = jnp.sin(ang)[:, None, :]

    for l in range(DEPTH):
        x = layer_norm(alpha * x + 0.5 * swiglu_ffn(x, ffn1_w_gate[l], ffn1_w_up[l], ffn1_w_down[l]),
                       ln1_g[l], ln1_b[l])
        m = hybrid_mixer(x, w_in[l], b_in[l], ret_gn_g[l], conv_k[l], conv_b[l], conv_ln_g[l],
                         conv_ln_b[l], w_ret_o[l], w_conv_o[l], w_out[l], cos, sin)
        x = layer_norm(alpha * x + m, ln2_g[l], ln2_b[l])
        x = layer_norm(alpha * x + 0.5 * swiglu_ffn(x, ffn2_w_gate[l], ffn2_w_up[l], ffn2_w_down[l]),
                       ln3_g[l], ln3_b[l])
    return x
```

```python
import functools
import math

import jax
import jax.numpy as jnp
import numpy as np
from jax import lax
from jax.experimental import pallas as pl
from jax.experimental.pallas import tpu as pltpu

RET_HEADS = 8
RET_DK = 128
RET_DV = 256
CONV_WIDTH = 31
ROPE_BASE = 10000.0
LN_EPS = 1e-5

V7X_LANES = 128
V7X_VMEM_BYTES = 64 * 1024 * 1024
VMEM_LIMIT_BYTES = V7X_VMEM_BYTES - 8 * 1024 * 1024

FFN_ROW_TILE = 512
MIX_ROW_TILE = 256
CONV_HALO = 32
CONV_ROW_BLOCK = 64


def _resident(shape):
    nd = len(shape)
    return pl.BlockSpec(shape, lambda *_: (0,) * nd, pipeline_mode=pl.Buffered(1))


def _layer_norm(y, g, b):
    mu = jnp.mean(y, axis=-1, keepdims=True)
    d = y - mu
    var = jnp.mean(d * d, axis=-1, keepdims=True)
    return d * lax.rsqrt(var + LN_EPS) * g + b


def _sigmoid(x):
    return 1.0 / (1.0 + jnp.exp(-x))


def _silu(x):
    return x * _sigmoid(x)


def _dot(a, b):
    return jnp.dot(a, b, preferred_element_type=jnp.float32)


def _ffn_ln_kernel(x_ref, wg_ref, wu_ref, wd_ref, g_ref, b_ref, o_ref, h_ref, *, alpha, n_chunks):
    x = x_ref[...]
    xb = x.astype(jnp.bfloat16)
    d_ff = wg_ref.shape[1]
    cw = d_ff // n_chunks
    for c in range(n_chunks):
        sl = slice(c * cw, (c + 1) * cw)
        gate = _dot(xb, wg_ref[:, sl])
        up = _dot(xb, wu_ref[:, sl])
        h_ref[:, sl] = (_silu(gate) * up).astype(jnp.bfloat16)
    y = alpha * x + 0.5 * _dot(h_ref[...], wd_ref[...])
    o_ref[...] = _layer_norm(y, g_ref[...], b_ref[...])


def _ffn_ln(x2d, w_gate, w_up, w_down, ln_g, ln_b, *, alpha):
    t, d = x2d.shape
    d_ff = w_gate.shape[1]
    tm = min(FFN_ROW_TILE, t)
    assert t % tm == 0
    n_chunks = 2 if d_ff % (2 * V7X_LANES) == 0 else 1
    kern = functools.partial(_ffn_ln_kernel, alpha=alpha, n_chunks=n_chunks)
    return pl.pallas_call(
        kern,
        grid=(t // tm,),
        in_specs=[
            pl.BlockSpec((tm, d), lambda i: (i, 0)),
            _resident((d, d_ff)),
            _resident((d, d_ff)),
            _resident((d_ff, d)),
            _resident((1, d)),
            _resident((1, d)),
        ],
        out_specs=pl.BlockSpec((tm, d), lambda i: (i, 0)),
        out_shape=jax.ShapeDtypeStruct((t, d), jnp.float32),
        scratch_shapes=[pltpu.VMEM((tm, d_ff), jnp.bfloat16)],
        compiler_params=pltpu.CompilerParams(
            dimension_semantics=("arbitrary",), vmem_limit_bytes=VMEM_LIMIT_BYTES),
        name="ffn_ln",
    )(x2d, w_gate.astype(jnp.bfloat16), w_up.astype(jnp.bfloat16), w_down.astype(jnp.bfloat16),
      ln_g.reshape(1, d), ln_b.reshape(1, d))


def _mixer_ln_kernel(x_ref, w_in_ref, b_in_ref, cosq_ref, sinq_ref, cosk_ref, sink_ref,
                     mask_ref, xi_ref, zeta_ref, cdec_ref, gn_g_ref, conv_k_ref, conv_b_ref,
                     cln_g_ref, cln_b_ref, w_ret_o_ref, w_conv_o_ref, w_out_ref, ln_g_ref, ln_b_ref,
                     o_ref,
                     state_ref, ubuf_ref, ybuf_ref, gated_ref,
                     *, alpha, d_model):
    ts = x_ref.shape[1]
    qk_w = RET_HEADS * RET_DK
    v_w = RET_HEADS * RET_DV
    n_cblk = d_model // V7X_LANES
    off_k = qk_w
    off_v = 2 * qk_w
    off_g = off_v + v_w
    off_a = off_g + v_w
    off_b = off_a + d_model
    off_gr = off_b + d_model
    off_gc = off_gr + d_model

    @pl.when(pl.program_id(1) == 0)
    def _():
        state_ref[...] = jnp.zeros_like(state_ref)
        ubuf_ref[:, 0:CONV_HALO, :] = jnp.zeros((n_cblk, CONV_HALO, V7X_LANES), jnp.float32)

    x = x_ref[0]
    xb = x.astype(jnp.bfloat16)

    def proj(off, width):
        return _dot(xb, w_in_ref[:, off:off + width]) + b_in_ref[:, off:off + width]

    cosq, sinq = cosq_ref[...], sinq_ref[...]
    cosk, sink = cosk_ref[...], sink_ref[...]
    for h in range(RET_HEADS):
        qh = proj(h * RET_DK, RET_DK)
        kh = proj(off_k + h * RET_DK, RET_DK)
        vh = proj(off_v + h * RET_DV, RET_DV).astype(jnp.bfloat16)
        gh = proj(off_g + h * RET_DV, RET_DV)
        qh = qh * cosq + pltpu.roll(qh, RET_DK // 2, 1) * sinq
        kh = kh * cosk + pltpu.roll(kh, RET_DK // 2, 1) * sink
        qb = qh.astype(jnp.bfloat16)
        kb = kh.astype(jnp.bfloat16)
        scores = lax.dot_general(qb, kb, (((1,), (1,)), ((), ())),
                                 preferred_element_type=jnp.float32) * mask_ref[h]
        state = state_ref[h]
        lhs = jnp.concatenate([scores.astype(jnp.bfloat16), (qh * xi_ref[h]).astype(jnp.bfloat16)], axis=1)
        rhs = jnp.concatenate([vh, state.astype(jnp.bfloat16)], axis=0)
        r = _dot(lhs, rhs)
        kz = (kh * zeta_ref[h]).astype(jnp.bfloat16)
        state_ref[h] = cdec_ref[h] * state + lax.dot_general(
            kz, vh, (((0,), (0,)), ((), ())), preferred_element_type=jnp.float32)
        mu = jnp.mean(r, axis=-1, keepdims=True)
        d = r - mu
        var = jnp.mean(d * d, axis=-1, keepdims=True)
        rn = d * lax.rsqrt(var + LN_EPS) * gn_g_ref[:, h * RET_DV:(h + 1) * RET_DV]
        gated_ref[:, h * RET_DV:(h + 1) * RET_DV] = (_silu(gh) * rn).astype(jnp.bfloat16)
    ret_out = _dot(gated_ref[...], w_ret_o_ref[...])

    u = proj(off_a, d_model) * _sigmoid(proj(off_b, d_model))
    for cb in range(n_cblk):
        ubuf_ref[cb, CONV_HALO:CONV_HALO + ts, :] = u[:, cb * V7X_LANES:(cb + 1) * V7X_LANES]

    def conv_block(cb, carry):
        kcb = conv_k_ref[cb]
        for rb in range(ts // CONV_ROW_BLOCK):
            base = rb * CONV_ROW_BLOCK + CONV_HALO - (CONV_WIDTH - 1)
            acc = jnp.zeros((CONV_ROW_BLOCK, V7X_LANES), jnp.float32)
            for j in range(CONV_WIDTH):
                acc = acc + ubuf_ref[cb, base + j:base + j + CONV_ROW_BLOCK, :] * kcb[j:j + 1, :]
            ybuf_ref[cb, rb * CONV_ROW_BLOCK:(rb + 1) * CONV_ROW_BLOCK, :] = acc
        ubuf_ref[cb, 0:CONV_HALO, :] = ubuf_ref[cb, ts:ts + CONV_HALO, :]
        return carry

    lax.fori_loop(0, n_cblk, conv_block, 0)
    y = jnp.concatenate([ybuf_ref[cb] for cb in range(n_cblk)], axis=1) + conv_b_ref[...]
    cu = _silu(_layer_norm(y, cln_g_ref[...], cln_b_ref[...]))
    conv_out = _dot(cu.astype(jnp.bfloat16), w_conv_o_ref[...])

    merged = _sigmoid(proj(off_gr, d_model)) * ret_out + _sigmoid(proj(off_gc, d_model)) * conv_out
    m = _dot(merged.astype(jnp.bfloat16), w_out_ref[...])
    o_ref[0] = _layer_norm(alpha * x + m, ln_g_ref[...], ln_b_ref[...])


def _retention_tables(seq, ts):
    half = RET_DK // 2
    freqs = ROPE_BASE ** (-jnp.arange(half, dtype=jnp.float32) / half)
    ang = jnp.arange(seq, dtype=jnp.float32)[:, None] * freqs[None, :]
    cos, sin = jnp.cos(ang), jnp.sin(ang)
    cos2 = jnp.concatenate([cos, cos], axis=1)
    sin2 = jnp.concatenate([-sin, sin], axis=1)
    qs = RET_DK ** -0.5
    log_g = jnp.log(1.0 - jnp.exp2(-5.0 - jnp.arange(RET_HEADS, dtype=jnp.float32)))
    idx = jnp.arange(ts, dtype=jnp.float32)
    diff = idx[:, None] - idx[None, :]
    mask = jnp.where(diff[None] >= 0, jnp.exp(jnp.maximum(diff, 0.0)[None] * log_g[:, None, None]), 0.0)
    xi = jnp.exp((idx[None, :] + 1.0) * log_g[:, None])[:, :, None]
    zeta = jnp.exp((ts - 1.0 - idx)[None, :] * log_g[:, None])[:, :, None]
    cdec = jnp.exp(ts * log_g)[:, None, None]
    return cos2 * qs, sin2 * qs, cos2, sin2, mask, xi, zeta, cdec


def _mixer_ln(x, w_in, b_in, ret_gn_g, conv_k, conv_b, conv_ln_g, conv_ln_b,
              w_ret_o, w_conv_o, w_out, ln_g, ln_b, *, alpha):
    bsz, seq, d = x.shape
    ts = min(MIX_ROW_TILE, seq)
    assert seq % ts == 0 and ts % CONV_ROW_BLOCK == 0 and d % V7X_LANES == 0
    in_w = w_in.shape[1]
    v_w = RET_HEADS * RET_DV
    n_cblk = d // V7X_LANES
    cosq, sinq, cosk, sink, mask, xi, zeta, cdec = _retention_tables(seq, ts)
    conv_k3 = conv_k.reshape(CONV_WIDTH, n_cblk, V7X_LANES).transpose(1, 0, 2)
    kern = functools.partial(_mixer_ln_kernel, alpha=alpha, d_model=d)
    pos_spec = pl.BlockSpec((ts, RET_DK), lambda b, i: (i, 0))
    return pl.pallas_call(
        kern,
        grid=(bsz, seq // ts),
        in_specs=[
            pl.BlockSpec((1, ts, d), lambda b, i: (b, i, 0)),
            _resident((d, in_w)),
            _resident((1, in_w)),
            pos_spec, pos_spec, pos_spec, pos_spec,
            _resident((RET_HEADS, ts, ts)),
            _resident((RET_HEADS, ts, 1)),
            _resident((RET_HEADS, ts, 1)),
            _resident((RET_HEADS, 1, 1)),
            _resident((1, v_w)),
            _resident((n_cblk, CONV_WIDTH, V7X_LANES)),
            _resident((1, d)),
            _resident((1, d)),
            _resident((1, d)),
            _resident((v_w, d)),
            _resident((d, d)),
            _resident((d, d)),
            _resident((1, d)),
            _resident((1, d)),
        ],
        out_specs=pl.BlockSpec((1, ts, d), lambda b, i: (b, i, 0)),
        out_shape=jax.ShapeDtypeStruct((bsz, seq, d), jnp.float32),
        scratch_shapes=[
            pltpu.VMEM((RET_HEADS, RET_DK, RET_DV), jnp.float32),
            pltpu.VMEM((n_cblk, CONV_HALO + ts, V7X_LANES), jnp.float32),
            pltpu.VMEM((n_cblk, ts, V7X_LANES), jnp.float32),
            pltpu.VMEM((ts, v_w), jnp.bfloat16),
        ],
        compiler_params=pltpu.CompilerParams(
            dimension_semantics=("arbitrary", "arbitrary"), vmem_limit_bytes=VMEM_LIMIT_BYTES),
        name="mixer_ln",
    )(x, w_in.astype(jnp.bfloat16), b_in.reshape(1, in_w), cosq, sinq, cosk, sink,
      mask, xi, zeta, cdec, ret_gn_g.reshape(1, v_w), conv_k3, conv_b.reshape(1, d),
      conv_ln_g.reshape(1, d), conv_ln_b.reshape(1, d), w_ret_o.astype(jnp.bfloat16),
      w_conv_o.astype(jnp.bfloat16), w_out.astype(jnp.bfloat16), ln_g.reshape(1, d), ln_b.reshape(1, d))


def kernel(x, ffn1_w_gate, ffn1_w_up, ffn1_w_down, ln1_g, ln1_b, w_in, b_in, ret_gn_g, conv_k, conv_b,
           conv_ln_g, conv_ln_b, w_ret_o, w_conv_o, w_out, ln2_g, ln2_b, ffn2_w_gate, ffn2_w_up,
           ffn2_w_down, ln3_g, ln3_b):
    depth = ffn1_w_gate.shape[0]
    alpha = (2.0 * depth) ** 0.25
    bsz, seq, d = x.shape
    for l in range(depth):
        x = _ffn_ln(x.reshape(bsz * seq, d), ffn1_w_gate[l], ffn1_w_up[l], ffn1_w_down[l],
                    ln1_g[l], ln1_b[l], alpha=alpha).reshape(bsz, seq, d)
        x = _mixer_ln(x, w_in[l], b_in[l], ret_gn_g[l], conv_k[l], conv_b[l], conv_ln_g[l], conv_ln_b[l],
                      w_ret_o[l], w_conv_o[l], w_out[l], ln2_g[l], ln2_b[l], alpha=alpha)
        x = _ffn_ln(x.reshape(bsz * seq, d), ffn2_w_gate[l], ffn2_w_up[l], ffn2_w_down[l],
                    ln3_g[l], ln3_b[l], alpha=alpha).reshape(bsz, seq, d)
    return x
```

```python
import functools
import math

import jax
import jax.numpy as jnp
import numpy as np
from jax import lax
from jax.experimental import pallas as pl
from jax.experimental.pallas import tpu as pltpu

RET_HEADS = 8
RET_DK = 128
RET_DV = 256
CONV_WIDTH = 31
ROPE_BASE = 10000.0
LN_EPS = 1e-5

V7X_LANES = 128
V7X_VMEM_BYTES = 64 * 1024 * 1024
VMEM_LIMIT_BYTES = V7X_VMEM_BYTES - 8 * 1024 * 1024

FFN_ROW_TILE = 512
MIX_ROW_TILE = 256
CONV_HALO = 32
CONV_ROW_BLOCK = 64


def _resident(shape):
    nd = len(shape)
    return pl.BlockSpec(shape, lambda *_: (0,) * nd, pipeline_mode=pl.Buffered(1))


def _layer_norm(y, g, b):
    mu = jnp.mean(y, axis=-1, keepdims=True)
    d = y - mu
    var = jnp.mean(d * d, axis=-1, keepdims=True)
    return d * lax.rsqrt(var + LN_EPS) * g + b


def _sigmoid(x):
    return 1.0 / (1.0 + jnp.exp(-x))


def _silu(x):
    return x * _sigmoid(x)


def _dot(a, b):
    return jnp.dot(a, b, preferred_element_type=jnp.float32)


def _pack_weight(w):
    k, n = w.shape
    wb = w.astype(jnp.bfloat16).reshape(k // 2, 2, n)
    return lax.bitcast_convert_type(jnp.swapaxes(wb, 1, 2), jnp.uint32)


def _wdot(a, w_packed):
    return _dot(a, pltpu.bitcast(w_packed, jnp.bfloat16))


def _ffn_ln_kernel(x_ref, wg_ref, wu_ref, wd_ref, g_ref, b_ref, o_ref, h_ref, *, alpha, n_chunks):
    x = x_ref[...]
    xb = x.astype(jnp.bfloat16)
    d_ff = wg_ref.shape[1]
    cw = d_ff // n_chunks
    for c in range(n_chunks):
        sl = slice(c * cw, (c + 1) * cw)
        gate = _wdot(xb, wg_ref[:, sl])
        up = _wdot(xb, wu_ref[:, sl])
        h_ref[:, sl] = (_silu(gate) * up).astype(jnp.bfloat16)
    y = alpha * x + 0.5 * _wdot(h_ref[...], wd_ref[...])
    o_ref[...] = _layer_norm(y, g_ref[...], b_ref[...])


def _ffn_ln(x2d, w_gate, w_up, w_down, ln_g, ln_b, *, alpha):
    t, d = x2d.shape
    d_ff = w_gate.shape[1]
    tm = min(FFN_ROW_TILE, t)
    assert t % tm == 0
    n_chunks = 2 if d_ff % (2 * V7X_LANES) == 0 else 1
    kern = functools.partial(_ffn_ln_kernel, alpha=alpha, n_chunks=n_chunks)
    return pl.pallas_call(
        kern,
        grid=(t // tm,),
        in_specs=[
            pl.BlockSpec((tm, d), lambda i: (i, 0)),
            _resident((d // 2, d_ff)),
            _resident((d // 2, d_ff)),
            _resident((d_ff // 2, d)),
            _resident((1, d)),
            _resident((1, d)),
        ],
        out_specs=pl.BlockSpec((tm, d), lambda i: (i, 0)),
        out_shape=jax.ShapeDtypeStruct((t, d), jnp.float32),
        scratch_shapes=[pltpu.VMEM((tm, d_ff), jnp.bfloat16)],
        compiler_params=pltpu.CompilerParams(
            dimension_semantics=("arbitrary",), vmem_limit_bytes=VMEM_LIMIT_BYTES),
        name="ffn_ln",
    )(x2d, _pack_weight(w_gate), _pack_weight(w_up), _pack_weight(w_down),
      ln_g.reshape(1, d), ln_b.reshape(1, d))


def _conv_channel_block(ubuf_ref, ybuf_ref, conv_k_ref, cb, ts):
    kcb = conv_k_ref[cb]
    for rb in range(ts // CONV_ROW_BLOCK):
        base = rb * CONV_ROW_BLOCK + CONV_HALO - (CONV_WIDTH - 1)
        acc = ubuf_ref[cb, base:base + CONV_ROW_BLOCK, :] * kcb[0:1, :]
        for j in range(1, CONV_WIDTH):
            acc = acc + ubuf_ref[cb, base + j:base + j + CONV_ROW_BLOCK, :] * kcb[j:j + 1, :]
        ybuf_ref[cb, rb * CONV_ROW_BLOCK:(rb + 1) * CONV_ROW_BLOCK, :] = acc
    ubuf_ref[cb, 0:CONV_HALO, :] = ubuf_ref[cb, ts:ts + CONV_HALO, :]


def _mixer_ln_kernel(x_ref, w_in_ref, b_in_ref, cosq_ref, sinq_ref, cosk_ref, sink_ref,
                     mask_ref, xi_ref, zeta_ref, cdec_ref, gn_g_ref, conv_k_ref, conv_b_ref,
                     cln_g_ref, cln_b_ref, w_ret_o_ref, w_conv_o_ref, w_out_ref, ln_g_ref, ln_b_ref,
                     o_ref,
                     state_ref, ubuf_ref, ybuf_ref, gated_ref, q_ref, qx_ref, k_ref, kz_ref, v_ref,
                     *, alpha, d_model):
    ts = x_ref.shape[1]
    qk_w = RET_HEADS * RET_DK
    v_w = RET_HEADS * RET_DV
    n_cblk = d_model // V7X_LANES
    off_k = qk_w
    off_v = 2 * qk_w
    off_g = off_v + v_w
    off_a = off_g + v_w
    off_b = off_a + d_model
    off_gr = off_b + d_model
    off_gc = off_gr + d_model

    @pl.when(pl.program_id(1) == 0)
    def _():
        state_ref[...] = jnp.zeros_like(state_ref)
        ubuf_ref[:, 0:CONV_HALO, :] = jnp.zeros((n_cblk, CONV_HALO, V7X_LANES), jnp.float32)

    x = x_ref[0]
    xb = x.astype(jnp.bfloat16)

    def proj(off, width):
        return _wdot(xb, w_in_ref[:, off:off + width]) + b_in_ref[:, off:off + width]

    u = proj(off_a, d_model) * _sigmoid(proj(off_b, d_model))
    for cb in range(n_cblk):
        ubuf_ref[cb, CONV_HALO:CONV_HALO + ts, :] = u[:, cb * V7X_LANES:(cb + 1) * V7X_LANES]

    cblocks = iter(range(n_cblk))

    def conv_some(n):
        for _ in range(n):
            cb = next(cblocks, None)
            if cb is not None:
                _conv_channel_block(ubuf_ref, ybuf_ref, conv_k_ref, cb, ts)

    per_stage = -(-n_cblk // 4)
    qa = proj(0, qk_w)
    conv_some(per_stage)
    cosq, sinq = cosq_ref[...], sinq_ref[...]
    for h in range(RET_HEADS):
        sl = slice(h * RET_DK, (h + 1) * RET_DK)
        qh = qa[:, sl]
        qh = qh * cosq + pltpu.roll(qh, RET_DK // 2, 1) * sinq
        q_ref[:, sl] = qh.astype(jnp.bfloat16)
        qx_ref[:, sl] = (qh * xi_ref[h]).astype(jnp.bfloat16)
    ka = proj(off_k, qk_w)
    conv_some(per_stage)
    cosk, sink = cosk_ref[...], sink_ref[...]
    for h in range(RET_HEADS):
        sl = slice(h * RET_DK, (h + 1) * RET_DK)
        kh = ka[:, sl]
        kh = kh * cosk + pltpu.roll(kh, RET_DK // 2, 1) * sink
        k_ref[:, sl] = kh.astype(jnp.bfloat16)
        kz_ref[:, sl] = (kh * zeta_ref[h]).astype(jnp.bfloat16)
    v_ref[:, 0:v_w // 2] = proj(off_v, v_w // 2).astype(jnp.bfloat16)
    conv_some(per_stage)
    v_ref[:, v_w // 2:v_w] = proj(off_v + v_w // 2, v_w // 2).astype(jnp.bfloat16)
    conv_some(n_cblk)

    for h in range(RET_HEADS):
        sk = slice(h * RET_DK, (h + 1) * RET_DK)
        sv = slice(h * RET_DV, (h + 1) * RET_DV)
        gh = proj(off_g + h * RET_DV, RET_DV)
        vh = v_ref[:, sv]
        scores = lax.dot_general(q_ref[:, sk], k_ref[:, sk], (((1,), (1,)), ((), ())),
                                 preferred_element_type=jnp.float32) * mask_ref[h]
        state = state_ref[h]
        lhs = jnp.concatenate([scores.astype(jnp.bfloat16), qx_ref[:, sk]], axis=1)
        rhs = jnp.concatenate([vh, state.astype(jnp.bfloat16)], axis=0)
        r = _dot(lhs, rhs)
        state_ref[h] = cdec_ref[h] * state + lax.dot_general(
            kz_ref[:, sk], vh, (((0,), (0,)), ((), ())), preferred_element_type=jnp.float32)
        mu = jnp.mean(r, axis=-1, keepdims=True)
        d = r - mu
        var = jnp.mean(d * d, axis=-1, keepdims=True)
        rn = d * lax.rsqrt(var + LN_EPS) * gn_g_ref[:, sv]
        gated_ref[:, sv] = (_silu(gh) * rn).astype(jnp.bfloat16)

    gate_r = _sigmoid(proj(off_gr, d_model))
    y = jnp.concatenate([ybuf_ref[cb] for cb in range(n_cblk)], axis=1) + conv_b_ref[...]
    cu = _silu(_layer_norm(y, cln_g_ref[...], cln_b_ref[...]))
    ret_out = _wdot(gated_ref[...], w_ret_o_ref[...])
    gate_c = _sigmoid(proj(off_gc, d_model))
    conv_out = _wdot(cu.astype(jnp.bfloat16), w_conv_o_ref[...])
    merged = gate_r * ret_out + gate_c * conv_out
    m = _wdot(merged.astype(jnp.bfloat16), w_out_ref[...])
    o_ref[0] = _layer_norm(alpha * x + m, ln_g_ref[...], ln_b_ref[...])


def _retention_tables(seq, ts):
    half = RET_DK // 2
    freqs = ROPE_BASE ** (-jnp.arange(half, dtype=jnp.float32) / half)
    ang = jnp.arange(seq, dtype=jnp.float32)[:, None] * freqs[None, :]
    cos, sin = jnp.cos(ang), jnp.sin(ang)
    cos2 = jnp.concatenate([cos, cos], axis=1)
    sin2 = jnp.concatenate([-sin, sin], axis=1)
    qs = RET_DK ** -0.5
    log_g = jnp.log(1.0 - jnp.exp2(-5.0 - jnp.arange(RET_HEADS, dtype=jnp.float32)))
    idx = jnp.arange(ts, dtype=jnp.float32)
    diff = idx[:, None] - idx[None, :]
    mask = jnp.where(diff[None] >= 0, jnp.exp(jnp.maximum(diff, 0.0)[None] * log_g[:, None, None]), 0.0)
    lanes = (RET_HEADS, ts, RET_DK)
    xi = jnp.broadcast_to(jnp.exp((idx[None, :] + 1.0) * log_g[:, None])[:, :, None], lanes)
    zeta = jnp.broadcast_to(jnp.exp((ts - 1.0 - idx)[None, :] * log_g[:, None])[:, :, None], lanes)
    cdec = jnp.exp(ts * log_g)[:, None, None]
    return cos2 * qs, sin2 * qs, cos2, sin2, mask, xi, zeta, cdec


def _mixer_ln(x, w_in, b_in, ret_gn_g, conv_k, conv_b, conv_ln_g, conv_ln_b,
              w_ret_o, w_conv_o, w_out, ln_g, ln_b, *, alpha):
    bsz, seq, d = x.shape
    ts = min(MIX_ROW_TILE, seq)
    assert seq % ts == 0 and ts % CONV_ROW_BLOCK == 0 and d % V7X_LANES == 0
    in_w = w_in.shape[1]
    v_w = RET_HEADS * RET_DV
    qk_w = RET_HEADS * RET_DK
    n_cblk = d // V7X_LANES
    cosq, sinq, cosk, sink, mask, xi, zeta, cdec = _retention_tables(seq, ts)
    conv_k3 = conv_k.reshape(CONV_WIDTH, n_cblk, V7X_LANES).transpose(1, 0, 2)
    kern = functools.partial(_mixer_ln_kernel, alpha=alpha, d_model=d)
    pos_spec = pl.BlockSpec((ts, RET_DK), lambda b, i: (i, 0))
    return pl.pallas_call(
        kern,
        grid=(bsz, seq // ts),
        in_specs=[
            pl.BlockSpec((1, ts, d), lambda b, i: (b, i, 0)),
            _resident((d // 2, in_w)),
            _resident((1, in_w)),
            pos_spec, pos_spec, pos_spec, pos_spec,
            _resident((RET_HEADS, ts, ts)),
            _resident((RET_HEADS, ts, RET_DK)),
            _resident((RET_HEADS, ts, RET_DK)),
            _resident((RET_HEADS, 1, 1)),
            _resident((1, v_w)),
            _resident((n_cblk, CONV_WIDTH, V7X_LANES)),
            _resident((1, d)),
            _resident((1, d)),
            _resident((1, d)),
            _resident((v_w // 2, d)),
            _resident((d // 2, d)),
            _resident((d // 2, d)),
            _resident((1, d)),
            _resident((1, d)),
        ],
        out_specs=pl.BlockSpec((1, ts, d), lambda b, i: (b, i, 0)),
        out_shape=jax.ShapeDtypeStruct((bsz, seq, d), jnp.float32),
        scratch_shapes=[
            pltpu.VMEM((RET_HEADS, RET_DK, RET_DV), jnp.float32),
            pltpu.VMEM((n_cblk, CONV_HALO + ts, V7X_LANES), jnp.float32),
            pltpu.VMEM((n_cblk, ts, V7X_LANES), jnp.float32),
            pltpu.VMEM((ts, v_w), jnp.bfloat16),
            pltpu.VMEM((ts, qk_w), jnp.bfloat16),
            pltpu.VMEM((ts, qk_w), jnp.bfloat16),
            pltpu.VMEM((ts, qk_w), jnp.bfloat16),
            pltpu.VMEM((ts, qk_w), jnp.bfloat16),
            pltpu.VMEM((ts, v_w), jnp.bfloat16),
        ],
        compiler_params=pltpu.CompilerParams(
            dimension_semantics=("arbitrary", "arbitrary"), vmem_limit_bytes=VMEM_LIMIT_BYTES),
        name="mixer_ln",
    )(x, _pack_weight(w_in), b_in.reshape(1, in_w), cosq, sinq, cosk, sink,
      mask, xi, zeta, cdec, ret_gn_g.reshape(1, v_w), conv_k3, conv_b.reshape(1, d),
      conv_ln_g.reshape(1, d), conv_ln_b.reshape(1, d), _pack_weight(w_ret_o),
      _pack_weight(w_conv_o), _pack_weight(w_out), ln_g.reshape(1, d), ln_b.reshape(1, d))


def kernel(x, ffn1_w_gate, ffn1_w_up, ffn1_w_down, ln1_g, ln1_b, w_in, b_in, ret_gn_g, conv_k, conv_b,
           conv_ln_g, conv_ln_b, w_ret_o, w_conv_o, w_out, ln2_g, ln2_b, ffn2_w_gate, ffn2_w_up,
           ffn2_w_down, ln3_g, ln3_b):
    depth = ffn1_w_gate.shape[0]
    alpha = (2.0 * depth) ** 0.25
    bsz, seq, d = x.shape
    for l in range(depth):
        x = _ffn_ln(x.reshape(bsz * seq, d), ffn1_w_gate[l], ffn1_w_up[l], ffn1_w_down[l],
                    ln1_g[l], ln1_b[l], alpha=alpha).reshape(bsz, seq, d)
        x = _mixer_ln(x, w_in[l], b_in[l], ret_gn_g[l], conv_k[l], conv_b[l], conv_ln_g[l], conv_ln_b[l],
                      w_ret_o[l], w_conv_o[l], w_out[l], ln2_g[l], ln2_b[l], alpha=alpha)
        x = _ffn_ln(x.reshape(bsz * seq, d), ffn2_w_gate[l], ffn2_w_up[l], ffn2_w_down[l],
                    ln3_g[l], ln3_b[l], alpha=alpha).reshape(bsz, seq, d)
    return x
```

```python
import functools
import math

import jax
import jax.numpy as jnp
import numpy as np
from jax import lax
from jax.experimental import pallas as pl
from jax.experimental.pallas import tpu as pltpu

RET_HEADS = 8
RET_DK = 128
RET_DV = 256
CONV_WIDTH = 31
ROPE_BASE = 10000.0
LN_EPS = 1e-5
LOG2_E = math.log2(math.e)

V7X_LANES = 128
V7X_VMEM_BYTES = 64 * 1024 * 1024
VMEM_LIMIT_BYTES = V7X_VMEM_BYTES - 8 * 1024 * 1024

FFN_ROW_TILE = 512
FFN_ROW_SUB = 128
MIX_ROW_TILE = 256
CONV_HALO = 32
CONV_ROW_BLOCK = 64


def _resident(shape):
    nd = len(shape)
    return pl.BlockSpec(shape, lambda *_: (0,) * nd, pipeline_mode=pl.Buffered(1))


def _layer_norm(y, g, b):
    mu = jnp.mean(y, axis=-1, keepdims=True)
    d = y - mu
    var = jnp.mean(d * d, axis=-1, keepdims=True)
    return d * lax.rsqrt(var + LN_EPS) * g + b


def _sigmoid(x):
    return 1.0 / (1.0 + jnp.exp2(x * (-LOG2_E)))


def _silu(x):
    return x * _sigmoid(x)


def _dot(a, b):
    return jnp.dot(a, b, preferred_element_type=jnp.float32)


def _bf16(w):
    return w.astype(jnp.bfloat16)


def _ffn_ln_kernel(x_ref, wg_ref, wu_ref, wd_ref, g_ref, b_ref, o_ref, h_ref, *, alpha, n_chunks, row_sub):
    d_ff = wg_ref.shape[1]
    cw = d_ff // n_chunks
    for r in range(x_ref.shape[0] // row_sub):
        rows = slice(r * row_sub, (r + 1) * row_sub)
        x = x_ref[rows, :]
        xb = x.astype(jnp.bfloat16)
        for c in range(n_chunks):
            sl = slice(c * cw, (c + 1) * cw)
            gate = _dot(xb, wg_ref[:, sl])
            up = _dot(xb, wu_ref[:, sl])
            h_ref[rows, sl] = (_silu(gate) * up).astype(jnp.bfloat16)
        y = alpha * x + 0.5 * _dot(h_ref[rows, :], wd_ref[...])
        o_ref[rows, :] = _layer_norm(y, g_ref[...], b_ref[...])


def _ffn_ln(x2d, w_gate, w_up, w_down, ln_g, ln_b, *, alpha):
    t, d = x2d.shape
    d_ff = w_gate.shape[1]
    tm = min(FFN_ROW_TILE, t)
    assert t % tm == 0 and tm % FFN_ROW_SUB == 0
    n_chunks = 2 if d_ff % (2 * V7X_LANES) == 0 else 1
    kern = functools.partial(_ffn_ln_kernel, alpha=alpha, n_chunks=n_chunks, row_sub=FFN_ROW_SUB)
    return pl.pallas_call(
        kern,
        grid=(t // tm,),
        in_specs=[
            pl.BlockSpec((tm, d), lambda i: (i, 0)),
            _resident((d, d_ff)),
            _resident((d, d_ff)),
            _resident((d_ff, d)),
            _resident((1, d)),
            _resident((1, d)),
        ],
        out_specs=pl.BlockSpec((tm, d), lambda i: (i, 0)),
        out_shape=jax.ShapeDtypeStruct((t, d), jnp.float32),
        scratch_shapes=[pltpu.VMEM((tm, d_ff), jnp.bfloat16)],
        compiler_params=pltpu.CompilerParams(
            dimension_semantics=("arbitrary",), vmem_limit_bytes=VMEM_LIMIT_BYTES),
        name="ffn_ln",
    )(x2d, _bf16(w_gate), _bf16(w_up), _bf16(w_down),
      ln_g.reshape(1, d), ln_b.reshape(1, d))


def _conv_channel_block(ubuf_ref, ybuf_ref, conv_k_ref, cb, ts):
    kcb = conv_k_ref[cb]
    for rb in range(ts // CONV_ROW_BLOCK):
        base = rb * CONV_ROW_BLOCK + CONV_HALO - (CONV_WIDTH - 1)
        acc = ubuf_ref[cb, base:base + CONV_ROW_BLOCK, :] * kcb[0:1, :]
        for j in range(1, CONV_WIDTH):
            acc = acc + ubuf_ref[cb, base + j:base + j + CONV_ROW_BLOCK, :] * kcb[j:j + 1, :]
        ybuf_ref[cb, rb * CONV_ROW_BLOCK:(rb + 1) * CONV_ROW_BLOCK, :] = acc
    ubuf_ref[cb, 0:CONV_HALO, :] = ubuf_ref[cb, ts:ts + CONV_HALO, :]


def _mixer_ln_kernel(x_ref, w_in_ref, b_in_ref, cosq_ref, sinq_ref, cosk_ref, sink_ref,
                     mask_ref, xi_ref, zeta_ref, cdec_ref, gn_g_ref, conv_k_ref, conv_b_ref,
                     cln_g_ref, cln_b_ref, w_ret_o_ref, w_conv_o_ref, w_out_ref, ln_g_ref, ln_b_ref,
                     o_ref,
                     state_ref, ubuf_ref, ybuf_ref, gated_ref, q_ref, qx_ref, k_ref, kz_ref, v_ref,
                     *, alpha, d_model):
    ts = x_ref.shape[1]
    qk_w = RET_HEADS * RET_DK
    v_w = RET_HEADS * RET_DV
    n_cblk = d_model // V7X_LANES
    off_k = qk_w
    off_v = 2 * qk_w
    off_g = off_v + v_w
    off_a = off_g + v_w
    off_b = off_a + d_model
    off_gr = off_b + d_model
    off_gc = off_gr + d_model

    @pl.when(pl.program_id(1) == 0)
    def _():
        state_ref[...] = jnp.zeros_like(state_ref)
        ubuf_ref[:, 0:CONV_HALO, :] = jnp.zeros((n_cblk, CONV_HALO, V7X_LANES), jnp.float32)

    x = x_ref[0]
    xb = x.astype(jnp.bfloat16)

    def proj(off, width):
        return _dot(xb, w_in_ref[:, off:off + width]) + b_in_ref[:, off:off + width]

    u = proj(off_a, d_model) * _sigmoid(proj(off_b, d_model))
    for cb in range(n_cblk):
        ubuf_ref[cb, CONV_HALO:CONV_HALO + ts, :] = u[:, cb * V7X_LANES:(cb + 1) * V7X_LANES]

    cblocks = iter(range(n_cblk))

    def conv_some(n):
        for _ in range(n):
            cb = next(cblocks, None)
            if cb is not None:
                _conv_channel_block(ubuf_ref, ybuf_ref, conv_k_ref, cb, ts)

    per_stage = -(-n_cblk // 4)
    qa = proj(0, qk_w)
    conv_some(per_stage)
    cosq, sinq = cosq_ref[...], sinq_ref[...]
    for h in range(RET_HEADS):
        sl = slice(h * RET_DK, (h + 1) * RET_DK)
        qh = qa[:, sl]
        qh = qh * cosq + pltpu.roll(qh, RET_DK // 2, 1) * sinq
        q_ref[:, sl] = qh.astype(jnp.bfloat16)
        qx_ref[:, sl] = (qh * xi_ref[h]).astype(jnp.bfloat16)
    ka = proj(off_k, qk_w)
    conv_some(per_stage)
    cosk, sink = cosk_ref[...], sink_ref[...]
    for h in range(RET_HEADS):
        sl = slice(h * RET_DK, (h + 1) * RET_DK)
        kh = ka[:, sl]
        kh = kh * cosk + pltpu.roll(kh, RET_DK // 2, 1) * sink
        k_ref[:, sl] = kh.astype(jnp.bfloat16)
        kz_ref[:, sl] = (kh * zeta_ref[h]).astype(jnp.bfloat16)
    v_ref[:, 0:v_w // 2] = proj(off_v, v_w // 2).astype(jnp.bfloat16)
    conv_some(per_stage)
    v_ref[:, v_w // 2:v_w] = proj(off_v + v_w // 2, v_w // 2).astype(jnp.bfloat16)
    conv_some(n_cblk)

    for h in range(RET_HEADS):
        sk = slice(h * RET_DK, (h + 1) * RET_DK)
        sv = slice(h * RET_DV, (h + 1) * RET_DV)
        gh = proj(off_g + h * RET_DV, RET_DV)
        vh = v_ref[:, sv]
        scores = lax.dot_general(q_ref[:, sk], k_ref[:, sk], (((1,), (1,)), ((), ())),
                                 preferred_element_type=jnp.float32) * mask_ref[h]
        state = state_ref[h]
        lhs = jnp.concatenate([scores.astype(jnp.bfloat16), qx_ref[:, sk]], axis=1)
        rhs = jnp.concatenate([vh, state.astype(jnp.bfloat16)], axis=0)
        r = _dot(lhs, rhs)
        state_ref[h] = cdec_ref[h] * state + lax.dot_general(
            kz_ref[:, sk], vh, (((0,), (0,)), ((), ())), preferred_element_type=jnp.float32)
        mu = jnp.mean(r, axis=-1, keepdims=True)
        d = r - mu
        var = jnp.mean(d * d, axis=-1, keepdims=True)
        rn = d * lax.rsqrt(var + LN_EPS) * gn_g_ref[:, sv]
        gated_ref[:, sv] = (_silu(gh) * rn).astype(jnp.bfloat16)

    gate_r = _sigmoid(proj(off_gr, d_model))
    y = jnp.concatenate([ybuf_ref[cb] for cb in range(n_cblk)], axis=1) + conv_b_ref[...]
    cu = _silu(_layer_norm(y, cln_g_ref[...], cln_b_ref[...]))
    ret_out = _dot(gated_ref[...], w_ret_o_ref[...])
    gate_c = _sigmoid(proj(off_gc, d_model))
    conv_out = _dot(cu.astype(jnp.bfloat16), w_conv_o_ref[...])
    merged = gate_r * ret_out + gate_c * conv_out
    m = _dot(merged.astype(jnp.bfloat16), w_out_ref[...])
    o_ref[0] = _layer_norm(alpha * x + m, ln_g_ref[...], ln_b_ref[...])


def _retention_tables(seq, ts):
    half = RET_DK // 2
    freqs = ROPE_BASE ** (-jnp.arange(half, dtype=jnp.float32) / half)
    ang = jnp.arange(seq, dtype=jnp.float32)[:, None] * freqs[None, :]
    cos, sin = jnp.cos(ang), jnp.sin(ang)
    cos2 = jnp.concatenate([cos, cos], axis=1)
    sin2 = jnp.concatenate([-sin, sin], axis=1)
    qs = RET_DK ** -0.5
    log_g = jnp.log(1.0 - jnp.exp2(-5.0 - jnp.arange(RET_HEADS, dtype=jnp.float32)))
    idx = jnp.arange(ts, dtype=jnp.float32)
    diff = idx[:, None] - idx[None, :]
    mask = jnp.where(diff[None] >= 0, jnp.exp(jnp.maximum(diff, 0.0)[None] * log_g[:, None, None]), 0.0)
    lanes = (RET_HEADS, ts, RET_DK)
    xi = jnp.broadcast_to(jnp.exp((idx[None, :] + 1.0) * log_g[:, None])[:, :, None], lanes)
    zeta = jnp.broadcast_to(jnp.exp((ts - 1.0 - idx)[None, :] * log_g[:, None])[:, :, None], lanes)
    cdec = jnp.exp(ts * log_g)[:, None, None]
    return cos2 * qs, sin2 * qs, cos2, sin2, mask, xi, zeta, cdec


def _mixer_ln(x, w_in, b_in, ret_gn_g, conv_k, conv_b, conv_ln_g, conv_ln_b,
              w_ret_o, w_conv_o, w_out, ln_g, ln_b, *, alpha):
    bsz, seq, d = x.shape
    ts = min(MIX_ROW_TILE, seq)
    assert seq % ts == 0 and ts % CONV_ROW_BLOCK == 0 and d % V7X_LANES == 0
    in_w = w_in.shape[1]
    v_w = RET_HEADS * RET_DV
    qk_w = RET_HEADS * RET_DK
    n_cblk = d // V7X_LANES
    cosq, sinq, cosk, sink, mask, xi, zeta, cdec = _retention_tables(seq, ts)
    conv_k3 = conv_k.reshape(CONV_WIDTH, n_cblk, V7X_LANES).transpose(1, 0, 2)
    kern = functools.partial(_mixer_ln_kernel, alpha=alpha, d_model=d)
    pos_spec = pl.BlockSpec((ts, RET_DK), lambda b, i: (i, 0))
    return pl.pallas_call(
        kern,
        grid=(bsz, seq // ts),
        in_specs=[
            pl.BlockSpec((1, ts, d), lambda b, i: (b, i, 0)),
            _resident((d, in_w)),
            _resident((1, in_w)),
            pos_spec, pos_spec, pos_spec, pos_spec,
            _resident((RET_HEADS, ts, ts)),
            _resident((RET_HEADS, ts, RET_DK)),
            _resident((RET_HEADS, ts, RET_DK)),
            _resident((RET_HEADS, 1, 1)),
            _resident((1, v_w)),
            _resident((n_cblk, CONV_WIDTH, V7X_LANES)),
            _resident((1, d)),
            _resident((1, d)),
            _resident((1, d)),
            _resident((v_w, d)),
            _resident((d, d)),
            _resident((d, d)),
            _resident((1, d)),
            _resident((1, d)),
        ],
        out_specs=pl.BlockSpec((1, ts, d), lambda b, i: (b, i, 0)),
        out_shape=jax.ShapeDtypeStruct((bsz, seq, d), jnp.float32),
        scratch_shapes=[
            pltpu.VMEM((RET_HEADS, RET_DK, RET_DV), jnp.float32),
            pltpu.VMEM((n_cblk, CONV_HALO + ts, V7X_LANES), jnp.float32),
            pltpu.VMEM((n_cblk, ts, V7X_LANES), jnp.float32),
            pltpu.VMEM((ts, v_w), jnp.bfloat16),
            pltpu.VMEM((ts, qk_w), jnp.bfloat16),
            pltpu.VMEM((ts, qk_w), jnp.bfloat16),
            pltpu.VMEM((ts, qk_w), jnp.bfloat16),
            pltpu.VMEM((ts, qk_w), jnp.bfloat16),
            pltpu.VMEM((ts, v_w), jnp.bfloat16),
        ],
        compiler_params=pltpu.CompilerParams(
            dimension_semantics=("arbitrary", "arbitrary"), vmem_limit_bytes=VMEM_LIMIT_BYTES),
        name="mixer_ln",
    )(x, _bf16(w_in), b_in.reshape(1, in_w), cosq, sinq, cosk, sink,
      mask, xi, zeta, cdec, ret_gn_g.reshape(1, v_w), conv_k3, conv_b.reshape(1, d),
      conv_ln_g.reshape(1, d), conv_ln_b.reshape(1, d), _bf16(w_ret_o),
      _bf16(w_conv_o), _bf16(w_out), ln_g.reshape(1, d), ln_b.reshape(1, d))


def kernel(x, ffn1_w_gate, ffn1_w_up, ffn1_w_down, ln1_g, ln1_b, w_in, b_in, ret_gn_g, conv_k, conv_b,
           conv_ln_g, conv_ln_b, w_ret_o, w_conv_o, w_out, ln2_g, ln2_b, ffn2_w_gate, ffn2_w_up,
           ffn2_w_down, ln3_g, ln3_b):
    depth = ffn1_w_gate.shape[0]
    alpha = (2.0 * depth) ** 0.25
    bsz, seq, d = x.shape
    for l in range(depth):
        x = _ffn_ln(x.reshape(bsz * seq, d), ffn1_w_gate[l], ffn1_w_up[l], ffn1_w_down[l],
                    ln1_g[l], ln1_b[l], alpha=alpha).reshape(bsz, seq, d)
        x = _mixer_ln(x, w_in[l], b_in[l], ret_gn_g[l], conv_k[l], conv_b[l], conv_ln_g[l], conv_ln_b[l],
                      w_ret_o[l], w_conv_o[l], w_out[l], ln2_g[l], ln2_b[l], alpha=alpha)
        x = _ffn_ln(x.reshape(bsz * seq, d), ffn2_w_gate[l], ffn2_w_up[l], ffn2_w_down[l],
                    ln3_g[l], ln3_b[l], alpha=alpha).reshape(bsz, seq, d)
    return x
```

```python
import functools
import math

import jax
import jax.numpy as jnp
import numpy as np
from jax import lax
from jax.experimental import pallas as pl
from jax.experimental.pallas import tpu as pltpu

RET_HEADS = 8
RET_DK = 128
RET_DV = 256
CONV_WIDTH = 31
ROPE_BASE = 10000.0
LN_EPS = 1e-5
LOG2_E = math.log2(math.e)

V7X_LANES = 128
V7X_VMEM_BYTES = 64 * 1024 * 1024
VMEM_LIMIT_BYTES = V7X_VMEM_BYTES - 8 * 1024 * 1024

FFN_ROW_TILE = 1024
FFN_ROW_SUB = 256
MIX_ROW_TILE = 256
CONV_HALO = 32
WEIGHT_CHUNK_ROWS = 64
CONV_ROW_BLOCK = 64


def _resident(shape):
    nd = len(shape)
    return pl.BlockSpec(shape, lambda *_: (0,) * nd, pipeline_mode=pl.Buffered(1))


def _layer_norm(y, g, b):
    mu = jnp.mean(y, axis=-1, keepdims=True)
    d = y - mu
    var = jnp.mean(d * d, axis=-1, keepdims=True)
    return d * lax.rsqrt(var + LN_EPS) * g + b


def _sigmoid(x):
    return 1.0 / (1.0 + jnp.exp2(x * (-LOG2_E)))


def _silu(x):
    return x * _sigmoid(x)


def _dot(a, b):
    return jnp.dot(a, b, preferred_element_type=jnp.float32)


def _load_weight_bf16(w_hbm, w_vmem, stage, sem, chunk_rows):
    n_chunks = w_hbm.shape[0] // chunk_rows

    def copy(c, slot):
        return pltpu.make_async_copy(w_hbm.at[pl.ds(c * chunk_rows, chunk_rows), :],
                                     stage.at[slot, 0:chunk_rows, 0:w_hbm.shape[1]], sem.at[slot])

    copy(0, 0).start()
    for c in range(n_chunks):
        slot = c % 2
        if c + 1 < n_chunks:
            copy(c + 1, 1 - slot).start()
        copy(c, slot).wait()
        w_vmem[c * chunk_rows:(c + 1) * chunk_rows, :] = (
            stage[slot, 0:chunk_rows, 0:w_hbm.shape[1]].astype(jnp.bfloat16))


def _ffn_ln_kernel(x_ref, wg_hbm, wu_hbm, wd_hbm, g_ref, b_ref, o_ref,
                   h_ref, wg_ref, wu_ref, wd_ref, stage_ref, sem, *, alpha, row_sub):
    @pl.when(pl.program_id(0) == 0)
    def _():
        for w_hbm, w_ref in ((wg_hbm, wg_ref), (wu_hbm, wu_ref), (wd_hbm, wd_ref)):
            _load_weight_bf16(w_hbm, w_ref, stage_ref, sem, WEIGHT_CHUNK_ROWS)

    for r in range(x_ref.shape[0] // row_sub):
        rows = slice(r * row_sub, (r + 1) * row_sub)
        x = x_ref[rows, :]
        xb = x.astype(jnp.bfloat16)
        h_ref[rows, :] = (_silu(_dot(xb, wg_ref[...])) * _dot(xb, wu_ref[...])).astype(jnp.bfloat16)
        y = alpha * x + 0.5 * _dot(h_ref[rows, :], wd_ref[...])
        o_ref[rows, :] = _layer_norm(y, g_ref[...], b_ref[...])


def _ffn_ln(x2d, w_gate, w_up, w_down, ln_g, ln_b, *, alpha):
    t, d = x2d.shape
    d_ff = w_gate.shape[1]
    tm = min(FFN_ROW_TILE, t)
    assert t % tm == 0 and tm % FFN_ROW_SUB == 0
    kern = functools.partial(_ffn_ln_kernel, alpha=alpha, row_sub=FFN_ROW_SUB)
    return pl.pallas_call(
        kern,
        grid=(t // tm,),
        in_specs=[
            pl.BlockSpec((tm, d), lambda i: (i, 0)),
            pl.BlockSpec(memory_space=pl.ANY),
            pl.BlockSpec(memory_space=pl.ANY),
            pl.BlockSpec(memory_space=pl.ANY),
            _resident((1, d)),
            _resident((1, d)),
        ],
        out_specs=pl.BlockSpec((tm, d), lambda i: (i, 0)),
        out_shape=jax.ShapeDtypeStruct((t, d), jnp.float32),
        scratch_shapes=[
            pltpu.VMEM((tm, d_ff), jnp.bfloat16),
            pltpu.VMEM((d, d_ff), jnp.bfloat16),
            pltpu.VMEM((d, d_ff), jnp.bfloat16),
            pltpu.VMEM((d_ff, d), jnp.bfloat16),
            pltpu.VMEM((2, WEIGHT_CHUNK_ROWS, max(d, d_ff)), jnp.float32),
            pltpu.SemaphoreType.DMA((2,)),
        ],
        compiler_params=pltpu.CompilerParams(
            dimension_semantics=("arbitrary",), vmem_limit_bytes=VMEM_LIMIT_BYTES),
        name="ffn_ln",
    )(x2d, w_gate, w_up, w_down,
      ln_g.reshape(1, d), ln_b.reshape(1, d))


def _conv_channel_block(ubuf_ref, ybuf_ref, conv_k_ref, cb, ts):
    kcb = conv_k_ref[cb]
    for rb in range(ts // CONV_ROW_BLOCK):
        base = rb * CONV_ROW_BLOCK + CONV_HALO - (CONV_WIDTH - 1)
        acc = ubuf_ref[cb, base:base + CONV_ROW_BLOCK, :] * kcb[0:1, :]
        for j in range(1, CONV_WIDTH):
            acc = acc + ubuf_ref[cb, base + j:base + j + CONV_ROW_BLOCK, :] * kcb[j:j + 1, :]
        ybuf_ref[cb, rb * CONV_ROW_BLOCK:(rb + 1) * CONV_ROW_BLOCK, :] = acc
    ubuf_ref[cb, 0:CONV_HALO, :] = ubuf_ref[cb, ts:ts + CONV_HALO, :]


def _mixer_ln_kernel(x_ref, w_in_hbm, b_in_ref, cosq_ref, sinq_ref, cosk_ref, sink_ref,
                     mask_ref, xi_ref, zeta_ref, cdec_ref, gn_g_ref, conv_k_ref, conv_b_ref,
                     cln_g_ref, cln_b_ref, w_ret_o_hbm, w_conv_o_hbm, w_out_hbm, ln_g_ref, ln_b_ref,
                     o_ref,
                     state_ref, ubuf_ref, ybuf_ref, gated_ref, q_ref, qx_ref, k_ref, kz_ref, v_ref,
                     w_in_ref, w_ret_o_ref, w_conv_o_ref, w_out_ref, stage_ref, sem,
                     *, alpha, d_model):
    ts = x_ref.shape[1]
    qk_w = RET_HEADS * RET_DK
    v_w = RET_HEADS * RET_DV
    n_cblk = d_model // V7X_LANES
    off_k = qk_w
    off_v = 2 * qk_w
    off_g = off_v + v_w
    off_a = off_g + v_w
    off_b = off_a + d_model
    off_gr = off_b + d_model
    off_gc = off_gr + d_model

    @pl.when((pl.program_id(0) == 0) & (pl.program_id(1) == 0))
    def _():
        for w_hbm, w_ref in ((w_in_hbm, w_in_ref), (w_ret_o_hbm, w_ret_o_ref),
                             (w_conv_o_hbm, w_conv_o_ref), (w_out_hbm, w_out_ref)):
            _load_weight_bf16(w_hbm, w_ref, stage_ref, sem, WEIGHT_CHUNK_ROWS)

    @pl.when(pl.program_id(1) == 0)
    def _():
        state_ref[...] = jnp.zeros_like(state_ref)
        ubuf_ref[:, 0:CONV_HALO, :] = jnp.zeros((n_cblk, CONV_HALO, V7X_LANES), jnp.float32)

    x = x_ref[0]
    xb = x.astype(jnp.bfloat16)

    def proj(off, width):
        return _dot(xb, w_in_ref[:, off:off + width]) + b_in_ref[:, off:off + width]

    u = proj(off_a, d_model) * _sigmoid(proj(off_b, d_model))
    for cb in range(n_cblk):
        ubuf_ref[cb, CONV_HALO:CONV_HALO + ts, :] = u[:, cb * V7X_LANES:(cb + 1) * V7X_LANES]

    cblocks = iter(range(n_cblk))

    def conv_some(n):
        for _ in range(n):
            cb = next(cblocks, None)
            if cb is not None:
                _conv_channel_block(ubuf_ref, ybuf_ref, conv_k_ref, cb, ts)

    per_stage = -(-n_cblk // 4)
    qa = proj(0, qk_w)
    conv_some(per_stage)
    cosq, sinq = cosq_ref[...], sinq_ref[...]
    for h in range(RET_HEADS):
        sl = slice(h * RET_DK, (h + 1) * RET_DK)
        qh = qa[:, sl]
        qh = qh * cosq + pltpu.roll(qh, RET_DK // 2, 1) * sinq
        q_ref[:, sl] = qh.astype(jnp.bfloat16)
        qx_ref[:, sl] = (qh * xi_ref[h]).astype(jnp.bfloat16)
    ka = proj(off_k, qk_w)
    conv_some(per_stage)
    cosk, sink = cosk_ref[...], sink_ref[...]
    for h in range(RET_HEADS):
        sl = slice(h * RET_DK, (h + 1) * RET_DK)
        kh = ka[:, sl]
        kh = kh * cosk + pltpu.roll(kh, RET_DK // 2, 1) * sink
        k_ref[:, sl] = kh.astype(jnp.bfloat16)
        kz_ref[:, sl] = (kh * zeta_ref[h]).astype(jnp.bfloat16)
    v_ref[:, 0:v_w // 2] = proj(off_v, v_w // 2).astype(jnp.bfloat16)
    conv_some(per_stage)
    v_ref[:, v_w // 2:v_w] = proj(off_v + v_w // 2, v_w // 2).astype(jnp.bfloat16)
    conv_some(n_cblk)

    for h in range(RET_HEADS):
        sk = slice(h * RET_DK, (h + 1) * RET_DK)
        sv = slice(h * RET_DV, (h + 1) * RET_DV)
        gh = proj(off_g + h * RET_DV, RET_DV)
        vh = v_ref[:, sv]
        scores = lax.dot_general(q_ref[:, sk], k_ref[:, sk], (((1,), (1,)), ((), ())),
                                 preferred_element_type=jnp.float32) * mask_ref[h]
        state = state_ref[h]
        lhs = jnp.concatenate([scores.astype(jnp.bfloat16), qx_ref[:, sk]], axis=1)
        rhs = jnp.concatenate([vh, state.astype(jnp.bfloat16)], axis=0)
        r = _dot(lhs, rhs)
        state_ref[h] = cdec_ref[h] * state + lax.dot_general(
            kz_ref[:, sk], vh, (((0,), (0,)), ((), ())), preferred_element_type=jnp.float32)
        mu = jnp.mean(r, axis=-1, keepdims=True)
        d = r - mu
        var = jnp.mean(d * d, axis=-1, keepdims=True)
        rn = d * lax.rsqrt(var + LN_EPS) * gn_g_ref[:, sv]
        gated_ref[:, sv] = (_silu(gh) * rn).astype(jnp.bfloat16)

    gate_r = _sigmoid(proj(off_gr, d_model))
    y = jnp.concatenate([ybuf_ref[cb] for cb in range(n_cblk)], axis=1) + conv_b_ref[...]
    cu = _silu(_layer_norm(y, cln_g_ref[...], cln_b_ref[...]))
    ret_out = _dot(gated_ref[...], w_ret_o_ref[...])
    gate_c = _sigmoid(proj(off_gc, d_model))
    conv_out = _dot(cu.astype(jnp.bfloat16), w_conv_o_ref[...])
    merged = gate_r * ret_out + gate_c * conv_out
    m = _dot(merged.astype(jnp.bfloat16), w_out_ref[...])
    o_ref[0] = _layer_norm(alpha * x + m, ln_g_ref[...], ln_b_ref[...])


def _retention_tables(seq, ts):
    half = RET_DK // 2
    freqs = ROPE_BASE ** (-jnp.arange(half, dtype=jnp.float32) / half)
    ang = jnp.arange(seq, dtype=jnp.float32)[:, None] * freqs[None, :]
    cos, sin = jnp.cos(ang), jnp.sin(ang)
    cos2 = jnp.concatenate([cos, cos], axis=1)
    sin2 = jnp.concatenate([-sin, sin], axis=1)
    qs = RET_DK ** -0.5
    log_g = jnp.log(1.0 - jnp.exp2(-5.0 - jnp.arange(RET_HEADS, dtype=jnp.float32)))
    idx = jnp.arange(ts, dtype=jnp.float32)
    diff = idx[:, None] - idx[None, :]
    mask = jnp.where(diff[None] >= 0, jnp.exp(jnp.maximum(diff, 0.0)[None] * log_g[:, None, None]), 0.0)
    lanes = (RET_HEADS, ts, RET_DK)
    xi = jnp.broadcast_to(jnp.exp((idx[None, :] + 1.0) * log_g[:, None])[:, :, None], lanes)
    zeta = jnp.broadcast_to(jnp.exp((ts - 1.0 - idx)[None, :] * log_g[:, None])[:, :, None], lanes)
    cdec = jnp.exp(ts * log_g)[:, None, None]
    return cos2 * qs, sin2 * qs, cos2, sin2, mask, xi, zeta, cdec


def _mixer_ln(x, w_in, b_in, ret_gn_g, conv_k, conv_b, conv_ln_g, conv_ln_b,
              w_ret_o, w_conv_o, w_out, ln_g, ln_b, *, alpha):
    bsz, seq, d = x.shape
    ts = min(MIX_ROW_TILE, seq)
    assert seq % ts == 0 and ts % CONV_ROW_BLOCK == 0 and d % V7X_LANES == 0
    in_w = w_in.shape[1]
    v_w = RET_HEADS * RET_DV
    qk_w = RET_HEADS * RET_DK
    n_cblk = d // V7X_LANES
    cosq, sinq, cosk, sink, mask, xi, zeta, cdec = _retention_tables(seq, ts)
    conv_k3 = conv_k.reshape(CONV_WIDTH, n_cblk, V7X_LANES).transpose(1, 0, 2)
    kern = functools.partial(_mixer_ln_kernel, alpha=alpha, d_model=d)
    pos_spec = pl.BlockSpec((ts, RET_DK), lambda b, i: (i, 0))
    return pl.pallas_call(
        kern,
        grid=(bsz, seq // ts),
        in_specs=[
            pl.BlockSpec((1, ts, d), lambda b, i: (b, i, 0)),
            pl.BlockSpec(memory_space=pl.ANY),
            _resident((1, in_w)),
            pos_spec, pos_spec, pos_spec, pos_spec,
            _resident((RET_HEADS, ts, ts)),
            _resident((RET_HEADS, ts, RET_DK)),
            _resident((RET_HEADS, ts, RET_DK)),
            _resident((RET_HEADS, 1, 1)),
            _resident((1, v_w)),
            _resident((n_cblk, CONV_WIDTH, V7X_LANES)),
            _resident((1, d)),
            _resident((1, d)),
            _resident((1, d)),
            pl.BlockSpec(memory_space=pl.ANY),
            pl.BlockSpec(memory_space=pl.ANY),
            pl.BlockSpec(memory_space=pl.ANY),
            _resident((1, d)),
            _resident((1, d)),
        ],
        out_specs=pl.BlockSpec((1, ts, d), lambda b, i: (b, i, 0)),
        out_shape=jax.ShapeDtypeStruct((bsz, seq, d), jnp.float32),
        scratch_shapes=[
            pltpu.VMEM((RET_HEADS, RET_DK, RET_DV), jnp.float32),
            pltpu.VMEM((n_cblk, CONV_HALO + ts, V7X_LANES), jnp.float32),
            pltpu.VMEM((n_cblk, ts, V7X_LANES), jnp.float32),
            pltpu.VMEM((ts, v_w), jnp.bfloat16),
            pltpu.VMEM((ts, qk_w), jnp.bfloat16),
            pltpu.VMEM((ts, qk_w), jnp.bfloat16),
            pltpu.VMEM((ts, qk_w), jnp.bfloat16),
            pltpu.VMEM((ts, qk_w), jnp.bfloat16),
            pltpu.VMEM((ts, v_w), jnp.bfloat16),
            pltpu.VMEM((d, in_w), jnp.bfloat16),
            pltpu.VMEM((v_w, d), jnp.bfloat16),
            pltpu.VMEM((d, d), jnp.bfloat16),
            pltpu.VMEM((d, d), jnp.bfloat16),
            pltpu.VMEM((2, WEIGHT_CHUNK_ROWS, in_w), jnp.float32),
            pltpu.SemaphoreType.DMA((2,)),
        ],
        compiler_params=pltpu.CompilerParams(
            dimension_semantics=("arbitrary", "arbitrary"), vmem_limit_bytes=VMEM_LIMIT_BYTES),
        name="mixer_ln",
    )(x, w_in, b_in.reshape(1, in_w), cosq, sinq, cosk, sink,
      mask, xi, zeta, cdec, ret_gn_g.reshape(1, v_w), conv_k3, conv_b.reshape(1, d),
      conv_ln_g.reshape(1, d), conv_ln_b.reshape(1, d), w_ret_o,
      w_conv_o, w_out, ln_g.reshape(1, d), ln_b.reshape(1, d))


def kernel(x, ffn1_w_gate, ffn1_w_up, ffn1_w_down, ln1_g, ln1_b, w_in, b_in, ret_gn_g, conv_k, conv_b,
           conv_ln_g, conv_ln_b, w_ret_o, w_conv_o, w_out, ln2_g, ln2_b, ffn2_w_gate, ffn2_w_up,
           ffn2_w_down, ln3_g, ln3_b):
    depth = ffn1_w_gate.shape[0]
    alpha = (2.0 * depth) ** 0.25
    bsz, seq, d = x.shape
    for l in range(depth):
        x = _ffn_ln(x.reshape(bsz * seq, d), ffn1_w_gate[l], ffn1_w_up[l], ffn1_w_down[l],
                    ln1_g[l], ln1_b[l], alpha=alpha).reshape(bsz, seq, d)
        x = _mixer_ln(x, w_in[l], b_in[l], ret_gn_g[l], conv_k[l], conv_b[l], conv_ln_g[l], conv_ln_b[l],
                      w_ret_o[l], w_conv_o[l], w_out[l], ln2_g[l], ln2_b[l], alpha=alpha)
        x = _ffn_ln(x.reshape(bsz * seq, d), ffn2_w_gate[l], ffn2_w_up[l], ffn2_w_down[l],
                    ln3_g[l], ln3_b[l], alpha=alpha).reshape(bsz, seq, d)
    return x
```

```python
import functools
import math

import jax
import jax.numpy as jnp
import numpy as np
from jax import lax
from jax.experimental import pallas as pl
from jax.experimental.pallas import tpu as pltpu

RET_HEADS = 8
RET_DK = 128
RET_DV = 256
CONV_WIDTH = 31
ROPE_BASE = 10000.0
LN_EPS = 1e-5
LOG2_E = math.log2(math.e)

V7X_LANES = 128
V7X_VMEM_BYTES = 64 * 1024 * 1024
VMEM_LIMIT_BYTES = V7X_VMEM_BYTES - 8 * 1024 * 1024

FFN_ROW_TILE = 1024
FFN_ROW_SUB = 256
MIX_ROW_TILE = 256
CONV_HALO = 32
CONV_ROW_BLOCK = 64


def _resident(shape):
    nd = len(shape)
    return pl.BlockSpec(shape, lambda *_: (0,) * nd, pipeline_mode=pl.Buffered(1))


def _layer_norm(y, g, b):
    mu = jnp.mean(y, axis=-1, keepdims=True)
    d = y - mu
    var = jnp.mean(d * d, axis=-1, keepdims=True)
    return d * lax.rsqrt(var + LN_EPS) * g + b


def _sigmoid(x):
    return 1.0 / (1.0 + jnp.exp2(x * (-LOG2_E)))


def _silu(x):
    return x * _sigmoid(x)


def _dot(a, b):
    return jnp.dot(a, b, preferred_element_type=jnp.float32)


def _bf16(w):
    return w.astype(jnp.bfloat16)


def _fetch_weights(pairs, sem):
    copies = [pltpu.make_async_copy(w_hbm, w_vmem, sem.at[i]) for i, (w_hbm, w_vmem) in enumerate(pairs)]
    for cp in copies:
        cp.start()
    for cp in copies:
        cp.wait()


def _ffn_ln_kernel(x_ref, wg_hbm, wu_hbm, wd_hbm, g_ref, b_ref, o_ref,
                   h_ref, wg_ref, wu_ref, wd_ref, sem, *, alpha, row_sub):
    @pl.when(pl.program_id(0) == 0)
    def _():
        _fetch_weights(((wg_hbm, wg_ref), (wu_hbm, wu_ref), (wd_hbm, wd_ref)), sem)

    for r in range(x_ref.shape[0] // row_sub):
        rows = slice(r * row_sub, (r + 1) * row_sub)
        x = x_ref[rows, :]
        xb = x.astype(jnp.bfloat16)
        h_ref[rows, :] = (_silu(_dot(xb, wg_ref[...])) * _dot(xb, wu_ref[...])).astype(jnp.bfloat16)
        y = alpha * x + 0.5 * _dot(h_ref[rows, :], wd_ref[...])
        o_ref[rows, :] = _layer_norm(y, g_ref[...], b_ref[...])


def _ffn_ln(x2d, w_gate, w_up, w_down, ln_g, ln_b, *, alpha):
    t, d = x2d.shape
    d_ff = w_gate.shape[1]
    tm = min(FFN_ROW_TILE, t)
    assert t % tm == 0 and tm % FFN_ROW_SUB == 0
    kern = functools.partial(_ffn_ln_kernel, alpha=alpha, row_sub=FFN_ROW_SUB)
    return pl.pallas_call(
        kern,
        grid=(t // tm,),
        in_specs=[
            pl.BlockSpec((tm, d), lambda i: (i, 0)),
            pl.BlockSpec(memory_space=pl.ANY),
            pl.BlockSpec(memory_space=pl.ANY),
            pl.BlockSpec(memory_space=pl.ANY),
            _resident((1, d)),
            _resident((1, d)),
        ],
        out_specs=pl.BlockSpec((tm, d), lambda i: (i, 0)),
        out_shape=jax.ShapeDtypeStruct((t, d), jnp.float32),
        scratch_shapes=[
            pltpu.VMEM((tm, d_ff), jnp.bfloat16),
            pltpu.VMEM((d, d_ff), jnp.bfloat16),
            pltpu.VMEM((d, d_ff), jnp.bfloat16),
            pltpu.VMEM((d_ff, d), jnp.bfloat16),
            pltpu.SemaphoreType.DMA((3,)),
        ],
        compiler_params=pltpu.CompilerParams(
            dimension_semantics=("arbitrary",), vmem_limit_bytes=VMEM_LIMIT_BYTES),
        name="ffn_ln",
    )(x2d, _bf16(w_gate), _bf16(w_up), _bf16(w_down),
      ln_g.reshape(1, d), ln_b.reshape(1, d))


def _conv_channel_block(ubuf_ref, ybuf_ref, conv_k_ref, cb, ts):
    kcb = conv_k_ref[cb]
    for rb in range(ts // CONV_ROW_BLOCK):
        base = rb * CONV_ROW_BLOCK + CONV_HALO - (CONV_WIDTH - 1)
        acc = ubuf_ref[cb, base:base + CONV_ROW_BLOCK, :] * kcb[0:1, :]
        for j in range(1, CONV_WIDTH):
            acc = acc + ubuf_ref[cb, base + j:base + j + CONV_ROW_BLOCK, :] * kcb[j:j + 1, :]
        ybuf_ref[cb, rb * CONV_ROW_BLOCK:(rb + 1) * CONV_ROW_BLOCK, :] = acc
    ubuf_ref[cb, 0:CONV_HALO, :] = ubuf_ref[cb, ts:ts + CONV_HALO, :]


def _mixer_ln_kernel(x_ref, w_in_hbm, b_in_ref, cosq_ref, sinq_ref, cosk_ref, sink_ref,
                     mask_ref, xi_ref, zeta_ref, cdec_ref, gn_g_ref, conv_k_ref, conv_b_ref,
                     cln_g_ref, cln_b_ref, w_ret_o_hbm, w_conv_o_hbm, w_out_hbm, ln_g_ref, ln_b_ref,
                     o_ref,
                     state_ref, ubuf_ref, ybuf_ref, gated_ref, q_ref, qx_ref, k_ref, kz_ref, v_ref,
                     w_in_ref, w_ret_o_ref, w_conv_o_ref, w_out_ref, sem,
                     *, alpha, d_model):
    ts = x_ref.shape[1]
    qk_w = RET_HEADS * RET_DK
    v_w = RET_HEADS * RET_DV
    n_cblk = d_model // V7X_LANES
    off_k = qk_w
    off_v = 2 * qk_w
    off_g = off_v + v_w
    off_a = off_g + v_w
    off_b = off_a + d_model
    off_gr = off_b + d_model
    off_gc = off_gr + d_model

    @pl.when((pl.program_id(0) == 0) & (pl.program_id(1) == 0))
    def _():
        _fetch_weights(((w_in_hbm, w_in_ref), (w_ret_o_hbm, w_ret_o_ref),
                        (w_conv_o_hbm, w_conv_o_ref), (w_out_hbm, w_out_ref)), sem)

    @pl.when(pl.program_id(1) == 0)
    def _():
        state_ref[...] = jnp.zeros_like(state_ref)
        ubuf_ref[:, 0:CONV_HALO, :] = jnp.zeros((n_cblk, CONV_HALO, V7X_LANES), jnp.float32)

    x = x_ref[0]
    xb = x.astype(jnp.bfloat16)

    def proj(off, width):
        return _dot(xb, w_in_ref[:, off:off + width]) + b_in_ref[:, off:off + width]

    u = proj(off_a, d_model) * _sigmoid(proj(off_b, d_model))
    for cb in range(n_cblk):
        ubuf_ref[cb, CONV_HALO:CONV_HALO + ts, :] = u[:, cb * V7X_LANES:(cb + 1) * V7X_LANES]

    cblocks = iter(range(n_cblk))

    def conv_some(n):
        for _ in range(n):
            cb = next(cblocks, None)
            if cb is not None:
                _conv_channel_block(ubuf_ref, ybuf_ref, conv_k_ref, cb, ts)

    per_stage = -(-n_cblk // 4)
    qa = proj(0, qk_w)
    conv_some(per_stage)
    cosq, sinq = cosq_ref[...], sinq_ref[...]
    for h in range(RET_HEADS):
        sl = slice(h * RET_DK, (h + 1) * RET_DK)
        qh = qa[:, sl]
        qh = qh * cosq + pltpu.roll(qh, RET_DK // 2, 1) * sinq
        q_ref[:, sl] = qh.astype(jnp.bfloat16)
        qx_ref[:, sl] = (qh * xi_ref[h]).astype(jnp.bfloat16)
    ka = proj(off_k, qk_w)
    conv_some(per_stage)
    cosk, sink = cosk_ref[...], sink_ref[...]
    for h in range(RET_HEADS):
        sl = slice(h * RET_DK, (h + 1) * RET_DK)
        kh = ka[:, sl]
        kh = kh * cosk + pltpu.roll(kh, RET_DK // 2, 1) * sink
        k_ref[:, sl] = kh.astype(jnp.bfloat16)
        kz_ref[:, sl] = (kh * zeta_ref[h]).astype(jnp.bfloat16)
    v_ref[:, 0:v_w // 2] = proj(off_v, v_w // 2).astype(jnp.bfloat16)
    conv_some(per_stage)
    v_ref[:, v_w // 2:v_w] = proj(off_v + v_w // 2, v_w // 2).astype(jnp.bfloat16)
    conv_some(n_cblk)

    for h in range(RET_HEADS):
        sk = slice(h * RET_DK, (h + 1) * RET_DK)
        sv = slice(h * RET_DV, (h + 1) * RET_DV)
        gh = proj(off_g + h * RET_DV, RET_DV)
        vh = v_ref[:, sv]
        scores = lax.dot_general(q_ref[:, sk], k_ref[:, sk], (((1,), (1,)), ((), ())),
                                 preferred_element_type=jnp.float32) * mask_ref[h]
        state = state_ref[h]
        lhs = jnp.concatenate([scores.astype(jnp.bfloat16), qx_ref[:, sk]], axis=1)
        rhs = jnp.concatenate([vh, state.astype(jnp.bfloat16)], axis=0)
        r = _dot(lhs, rhs)
        state_ref[h] = cdec_ref[h] * state + lax.dot_general(
            kz_ref[:, sk], vh, (((0,), (0,)), ((), ())), preferred_element_type=jnp.float32)
        mu = jnp.mean(r, axis=-1, keepdims=True)
        d = r - mu
        var = jnp.mean(d * d, axis=-1, keepdims=True)
        rn = d * lax.rsqrt(var + LN_EPS) * gn_g_ref[:, sv]
        gated_ref[:, sv] = (_silu(gh) * rn).astype(jnp.bfloat16)

    gate_r = _sigmoid(proj(off_gr, d_model))
    y = jnp.concatenate([ybuf_ref[cb] for cb in range(n_cblk)], axis=1) + conv_b_ref[...]
    cu = _silu(_layer_norm(y, cln_g_ref[...], cln_b_ref[...]))
    ret_out = _dot(gated_ref[...], w_ret_o_ref[...])
    gate_c = _sigmoid(proj(off_gc, d_model))
    conv_out = _dot(cu.astype(jnp.bfloat16), w_conv_o_ref[...])
    merged = gate_r * ret_out + gate_c * conv_out
    m = _dot(merged.astype(jnp.bfloat16), w_out_ref[...])
    o_ref[0] = _layer_norm(alpha * x + m, ln_g_ref[...], ln_b_ref[...])


def _retention_tables(seq, ts):
    half = RET_DK // 2
    freqs = ROPE_BASE ** (-jnp.arange(half, dtype=jnp.float32) / half)
    ang = jnp.arange(seq, dtype=jnp.float32)[:, None] * freqs[None, :]
    cos, sin = jnp.cos(ang), jnp.sin(ang)
    cos2 = jnp.concatenate([cos, cos], axis=1)
    sin2 = jnp.concatenate([-sin, sin], axis=1)
    qs = RET_DK ** -0.5
    log_g = jnp.log(1.0 - jnp.exp2(-5.0 - jnp.arange(RET_HEADS, dtype=jnp.float32)))
    idx = jnp.arange(ts, dtype=jnp.float32)
    diff = idx[:, None] - idx[None, :]
    mask = jnp.where(diff[None] >= 0, jnp.exp(jnp.maximum(diff, 0.0)[None] * log_g[:, None, None]), 0.0)
    lanes = (RET_HEADS, ts, RET_DK)
    xi = jnp.broadcast_to(jnp.exp((idx[None, :] + 1.0) * log_g[:, None])[:, :, None], lanes)
    zeta = jnp.broadcast_to(jnp.exp((ts - 1.0 - idx)[None, :] * log_g[:, None])[:, :, None], lanes)
    cdec = jnp.exp(ts * log_g)[:, None, None]
    return cos2 * qs, sin2 * qs, cos2, sin2, mask, xi, zeta, cdec


def _mixer_ln(x, w_in, b_in, ret_gn_g, conv_k, conv_b, conv_ln_g, conv_ln_b,
              w_ret_o, w_conv_o, w_out, ln_g, ln_b, *, alpha):
    bsz, seq, d = x.shape
    ts = min(MIX_ROW_TILE, seq)
    assert seq % ts == 0 and ts % CONV_ROW_BLOCK == 0 and d % V7X_LANES == 0
    in_w = w_in.shape[1]
    v_w = RET_HEADS * RET_DV
    qk_w = RET_HEADS * RET_DK
    n_cblk = d // V7X_LANES
    cosq, sinq, cosk, sink, mask, xi, zeta, cdec = _retention_tables(seq, ts)
    conv_k3 = conv_k.reshape(CONV_WIDTH, n_cblk, V7X_LANES).transpose(1, 0, 2)
    kern = functools.partial(_mixer_ln_kernel, alpha=alpha, d_model=d)
    pos_spec = pl.BlockSpec((ts, RET_DK), lambda b, i: (i, 0))
    return pl.pallas_call(
        kern,
        grid=(bsz, seq // ts),
        in_specs=[
            pl.BlockSpec((1, ts, d), lambda b, i: (b, i, 0)),
            pl.BlockSpec(memory_space=pl.ANY),
            _resident((1, in_w)),
            pos_spec, pos_spec, pos_spec, pos_spec,
            _resident((RET_HEADS, ts, ts)),
            _resident((RET_HEADS, ts, RET_DK)),
            _resident((RET_HEADS, ts, RET_DK)),
            _resident((RET_HEADS, 1, 1)),
            _resident((1, v_w)),
            _resident((n_cblk, CONV_WIDTH, V7X_LANES)),
            _resident((1, d)),
            _resident((1, d)),
            _resident((1, d)),
            pl.BlockSpec(memory_space=pl.ANY),
            pl.BlockSpec(memory_space=pl.ANY),
            pl.BlockSpec(memory_space=pl.ANY),
            _resident((1, d)),
            _resident((1, d)),
        ],
        out_specs=pl.BlockSpec((1, ts, d), lambda b, i: (b, i, 0)),
        out_shape=jax.ShapeDtypeStruct((bsz, seq, d), jnp.float32),
        scratch_shapes=[
            pltpu.VMEM((RET_HEADS, RET_DK, RET_DV), jnp.float32),
            pltpu.VMEM((n_cblk, CONV_HALO + ts, V7X_LANES), jnp.float32),
            pltpu.VMEM((n_cblk, ts, V7X_LANES), jnp.float32),
            pltpu.VMEM((ts, v_w), jnp.bfloat16),
            pltpu.VMEM((ts, qk_w), jnp.bfloat16),
            pltpu.VMEM((ts, qk_w), jnp.bfloat16),
            pltpu.VMEM((ts, qk_w), jnp.bfloat16),
            pltpu.VMEM((ts, qk_w), jnp.bfloat16),
            pltpu.VMEM((ts, v_w), jnp.bfloat16),
            pltpu.VMEM((d, in_w), jnp.bfloat16),
            pltpu.VMEM((v_w, d), jnp.bfloat16),
            pltpu.VMEM((d, d), jnp.bfloat16),
            pltpu.VMEM((d, d), jnp.bfloat16),
            pltpu.SemaphoreType.DMA((4,)),
        ],
        compiler_params=pltpu.CompilerParams(
            dimension_semantics=("arbitrary", "arbitrary"), vmem_limit_bytes=VMEM_LIMIT_BYTES),
        name="mixer_ln",
    )(x, _bf16(w_in), b_in.reshape(1, in_w), cosq, sinq, cosk, sink,
      mask, xi, zeta, cdec, ret_gn_g.reshape(1, v_w), conv_k3, conv_b.reshape(1, d),
      conv_ln_g.reshape(1, d), conv_ln_b.reshape(1, d), _bf16(w_ret_o),
      _bf16(w_conv_o), _bf16(w_out), ln_g.reshape(1, d), ln_b.reshape(1, d))


def kernel(x, ffn1_w_gate, ffn1_w_up, ffn1_w_down, ln1_g, ln1_b, w_in, b_in, ret_gn_g, conv_k, conv_b,
           conv_ln_g, conv_ln_b, w_ret_o, w_conv_o, w_out, ln2_g, ln2_b, ffn2_w_gate, ffn2_w_up,
           ffn2_w_down, ln3_g, ln3_b):
    depth = ffn1_w_gate.shape[0]
    alpha = (2.0 * depth) ** 0.25
    bsz, seq, d = x.shape
    for l in range(depth):
        x = _ffn_ln(x.reshape(bsz * seq, d), ffn1_w_gate[l], ffn1_w_up[l], ffn1_w_down[l],
                    ln1_g[l], ln1_b[l], alpha=alpha).reshape(bsz, seq, d)
        x = _mixer_ln(x, w_in[l], b_in[l], ret_gn_g[l], conv_k[l], conv_b[l], conv_ln_g[l], conv_ln_b[l],
                      w_ret_o[l], w_conv_o[l], w_out[l], ln2_g[l], ln2_b[l], alpha=alpha)
        x = _ffn_ln(x.reshape(bsz * seq, d), ffn2_w_gate[l], ffn2_w_up[l], ffn2_w_down[l],
                    ln3_g[l], ln3_b[l], alpha=alpha).reshape(bsz, seq, d)
    return x
```

```python
import functools
import math

import jax
import jax.numpy as jnp
import numpy as np
from jax import lax
from jax.experimental import pallas as pl
from jax.experimental.pallas import tpu as pltpu

RET_HEADS = 8
RET_DK = 128
RET_DV = 256
CONV_WIDTH = 31
ROPE_BASE = 10000.0
LN_EPS = 1e-5
LOG2_E = math.log2(math.e)

V7X_LANES = 128
V7X_VMEM_BYTES = 64 * 1024 * 1024
VMEM_LIMIT_BYTES = V7X_VMEM_BYTES - 8 * 1024 * 1024

FFN_ROW_TILE = 1024
FFN_ROW_SUB = 256
MIX_ROW_TILE = 256
CONV_HALO = 32
CONV_ROW_BLOCK = 128


def _resident(shape):
    nd = len(shape)
    return pl.BlockSpec(shape, lambda *_: (0,) * nd, pipeline_mode=pl.Buffered(1))


def _layer_norm(y, g, b):
    mu = jnp.mean(y, axis=-1, keepdims=True)
    d = y - mu
    var = jnp.mean(d * d, axis=-1, keepdims=True)
    return d * lax.rsqrt(var + LN_EPS) * g + b


def _sigmoid(x):
    return 1.0 / (1.0 + jnp.exp2(x * (-LOG2_E)))


def _silu(x):
    return x * _sigmoid(x)


def _dot(a, b):
    return jnp.dot(a, b, preferred_element_type=jnp.float32)


def _bf16(w):
    return w.astype(jnp.bfloat16)


def _fetch_weights(pairs, sem):
    copies = [pltpu.make_async_copy(w_hbm, w_vmem, sem.at[i]) for i, (w_hbm, w_vmem) in enumerate(pairs)]
    for cp in copies:
        cp.start()
    for cp in copies:
        cp.wait()


def _ffn_ln_kernel(x_ref, wg_hbm, wu_hbm, wd_hbm, g_ref, b_ref, o_ref,
                   h_ref, wg_ref, wu_ref, wd_ref, sem, *, alpha, row_sub):
    @pl.when(pl.program_id(0) == 0)
    def _():
        _fetch_weights(((wg_hbm, wg_ref), (wu_hbm, wu_ref), (wd_hbm, wd_ref)), sem)

    for r in range(x_ref.shape[0] // row_sub):
        rows = slice(r * row_sub, (r + 1) * row_sub)
        x = x_ref[rows, :]
        xb = x.astype(jnp.bfloat16)
        h_ref[rows, :] = (_silu(_dot(xb, wg_ref[...])) * _dot(xb, wu_ref[...])).astype(jnp.bfloat16)
        y = alpha * x + 0.5 * _dot(h_ref[rows, :], wd_ref[...])
        o_ref[rows, :] = _layer_norm(y, g_ref[...], b_ref[...])


def _ffn_ln(x2d, w_gate, w_up, w_down, ln_g, ln_b, *, alpha):
    t, d = x2d.shape
    d_ff = w_gate.shape[1]
    tm = min(FFN_ROW_TILE, t)
    assert t % tm == 0 and tm % FFN_ROW_SUB == 0
    kern = functools.partial(_ffn_ln_kernel, alpha=alpha, row_sub=FFN_ROW_SUB)
    return pl.pallas_call(
        kern,
        grid=(t // tm,),
        in_specs=[
            pl.BlockSpec((tm, d), lambda i: (i, 0)),
            pl.BlockSpec(memory_space=pl.ANY),
            pl.BlockSpec(memory_space=pl.ANY),
            pl.BlockSpec(memory_space=pl.ANY),
            _resident((1, d)),
            _resident((1, d)),
        ],
        out_specs=pl.BlockSpec((tm, d), lambda i: (i, 0)),
        out_shape=jax.ShapeDtypeStruct((t, d), jnp.float32),
        scratch_shapes=[
            pltpu.VMEM((tm, d_ff), jnp.bfloat16),
            pltpu.VMEM((d, d_ff), jnp.bfloat16),
            pltpu.VMEM((d, d_ff), jnp.bfloat16),
            pltpu.VMEM((d_ff, d), jnp.bfloat16),
            pltpu.SemaphoreType.DMA((3,)),
        ],
        compiler_params=pltpu.CompilerParams(
            dimension_semantics=("arbitrary",), vmem_limit_bytes=VMEM_LIMIT_BYTES),
        name="ffn_ln",
    )(x2d, _bf16(w_gate), _bf16(w_up), _bf16(w_down),
      ln_g.reshape(1, d), ln_b.reshape(1, d))


def _conv_channel_block(ubuf_ref, ybuf_ref, conv_k_ref, cb, ts):
    kcb = conv_k_ref[cb]
    for rb in range(ts // CONV_ROW_BLOCK):
        base = rb * CONV_ROW_BLOCK + CONV_HALO - (CONV_WIDTH - 1)
        acc = ubuf_ref[cb, base:base + CONV_ROW_BLOCK, :] * kcb[0:1, :]
        for j in range(1, CONV_WIDTH):
            acc = acc + ubuf_ref[cb, base + j:base + j + CONV_ROW_BLOCK, :] * kcb[j:j + 1, :]
        ybuf_ref[cb, rb * CONV_ROW_BLOCK:(rb + 1) * CONV_ROW_BLOCK, :] = acc
    ubuf_ref[cb, 0:CONV_HALO, :] = ubuf_ref[cb, ts:ts + CONV_HALO, :]


def _mixer_ln_kernel(x_ref, w_in_hbm, b_in_ref, cosq_ref, sinq_ref, cosk_ref, sink_ref,
                     mask_ref, xi_ref, zeta_ref, cdec_ref, gn_g_ref, conv_k_ref, conv_b_ref,
                     cln_g_ref, cln_b_ref, w_ret_o_hbm, w_conv_o_hbm, w_out_hbm, ln_g_ref, ln_b_ref,
                     o_ref,
                     state_ref, ubuf_ref, ybuf_ref, gated_ref, q_ref, qx_ref, k_ref, kz_ref, v_ref,
                     w_in_ref, w_ret_o_ref, w_conv_o_ref, w_out_ref, sem,
                     *, alpha, d_model):
    ts = x_ref.shape[1]
    qk_w = RET_HEADS * RET_DK
    v_w = RET_HEADS * RET_DV
    n_cblk = d_model // V7X_LANES
    off_k = qk_w
    off_v = 2 * qk_w
    off_g = off_v + v_w
    off_a = off_g + v_w
    off_b = off_a + d_model
    off_gr = off_b + d_model
    off_gc = off_gr + d_model

    @pl.when((pl.program_id(0) == 0) & (pl.program_id(1) == 0))
    def _():
        _fetch_weights(((w_in_hbm, w_in_ref), (w_ret_o_hbm, w_ret_o_ref),
                        (w_conv_o_hbm, w_conv_o_ref), (w_out_hbm, w_out_ref)), sem)

    @pl.when(pl.program_id(1) == 0)
    def _():
        state_ref[...] = jnp.zeros_like(state_ref)
        ubuf_ref[:, 0:CONV_HALO, :] = jnp.zeros((n_cblk, CONV_HALO, V7X_LANES), jnp.float32)

    x = x_ref[0]
    xb = x.astype(jnp.bfloat16)

    def proj(off, width):
        return _dot(xb, w_in_ref[:, off:off + width]) + b_in_ref[:, off:off + width]

    u = proj(off_a, d_model) * _sigmoid(proj(off_b, d_model))
    for cb in range(n_cblk):
        ubuf_ref[cb, CONV_HALO:CONV_HALO + ts, :] = u[:, cb * V7X_LANES:(cb + 1) * V7X_LANES]

    cblocks = iter(range(n_cblk))

    def conv_some(n):
        for _ in range(n):
            cb = next(cblocks, None)
            if cb is not None:
                _conv_channel_block(ubuf_ref, ybuf_ref, conv_k_ref, cb, ts)

    per_stage = -(-n_cblk // 4)
    qa = proj(0, qk_w)
    conv_some(per_stage)
    cosq, sinq = cosq_ref[...], sinq_ref[...]
    for h in range(RET_HEADS):
        sl = slice(h * RET_DK, (h + 1) * RET_DK)
        qh = qa[:, sl]
        qh = qh * cosq + pltpu.roll(qh, RET_DK // 2, 1) * sinq
        q_ref[:, sl] = qh.astype(jnp.bfloat16)
        qx_ref[:, sl] = (qh * xi_ref[h]).astype(jnp.bfloat16)
    ka = proj(off_k, qk_w)
    conv_some(per_stage)
    cosk, sink = cosk_ref[...], sink_ref[...]
    for h in range(RET_HEADS):
        sl = slice(h * RET_DK, (h + 1) * RET_DK)
        kh = ka[:, sl]
        kh = kh * cosk + pltpu.roll(kh, RET_DK // 2, 1) * sink
        k_ref[:, sl] = kh.astype(jnp.bfloat16)
        kz_ref[:, sl] = (kh * zeta_ref[h]).astype(jnp.bfloat16)
    v_ref[:, 0:v_w // 2] = proj(off_v, v_w // 2).astype(jnp.bfloat16)
    conv_some(per_stage)
    v_ref[:, v_w // 2:v_w] = proj(off_v + v_w // 2, v_w // 2).astype(jnp.bfloat16)
    conv_some(n_cblk)

    for h in range(RET_HEADS):
        sk = slice(h * RET_DK, (h + 1) * RET_DK)
        sv = slice(h * RET_DV, (h + 1) * RET_DV)
        gh = proj(off_g + h * RET_DV, RET_DV)
        vh = v_ref[:, sv]
        scores = lax.dot_general(q_ref[:, sk], k_ref[:, sk], (((1,), (1,)), ((), ())),
                                 preferred_element_type=jnp.float32) * mask_ref[h]
        state = state_ref[h]
        lhs = jnp.concatenate([scores.astype(jnp.bfloat16), qx_ref[:, sk]], axis=1)
        rhs = jnp.concatenate([vh, state.astype(jnp.bfloat16)], axis=0)
        r = _dot(lhs, rhs)
        state_ref[h] = cdec_ref[h] * state + lax.dot_general(
            kz_ref[:, sk], vh, (((0,), (0,)), ((), ())), preferred_element_type=jnp.float32)
        mu = jnp.mean(r, axis=-1, keepdims=True)
        d = r - mu
        var = jnp.mean(d * d, axis=-1, keepdims=True)
        rn = d * lax.rsqrt(var + LN_EPS) * gn_g_ref[:, sv]
        gated_ref[:, sv] = (_silu(gh) * rn).astype(jnp.bfloat16)

    gate_r = _sigmoid(proj(off_gr, d_model))
    y = jnp.concatenate([ybuf_ref[cb] for cb in range(n_cblk)], axis=1) + conv_b_ref[...]
    cu = _silu(_layer_norm(y, cln_g_ref[...], cln_b_ref[...]))
    ret_out = _dot(gated_ref[...], w_ret_o_ref[...])
    gate_c = _sigmoid(proj(off_gc, d_model))
    conv_out = _dot(cu.astype(jnp.bfloat16), w_conv_o_ref[...])
    merged = gate_r * ret_out + gate_c * conv_out
    m = _dot(merged.astype(jnp.bfloat16), w_out_ref[...])
    o_ref[0] = _layer_norm(alpha * x + m, ln_g_ref[...], ln_b_ref[...])


def _retention_tables(seq, ts):
    half = RET_DK // 2
    freqs = ROPE_BASE ** (-jnp.arange(half, dtype=jnp.float32) / half)
    ang = jnp.arange(seq, dtype=jnp.float32)[:, None] * freqs[None, :]
    cos, sin = jnp.cos(ang), jnp.sin(ang)
    cos2 = jnp.concatenate([cos, cos], axis=1)
    sin2 = jnp.concatenate([-sin, sin], axis=1)
    qs = RET_DK ** -0.5
    log_g = jnp.log(1.0 - jnp.exp2(-5.0 - jnp.arange(RET_HEADS, dtype=jnp.float32)))
    idx = jnp.arange(ts, dtype=jnp.float32)
    diff = idx[:, None] - idx[None, :]
    mask = jnp.where(diff[None] >= 0, jnp.exp(jnp.maximum(diff, 0.0)[None] * log_g[:, None, None]), 0.0)
    lanes = (RET_HEADS, ts, RET_DK)
    xi = jnp.broadcast_to(jnp.exp((idx[None, :] + 1.0) * log_g[:, None])[:, :, None], lanes)
    zeta = jnp.broadcast_to(jnp.exp((ts - 1.0 - idx)[None, :] * log_g[:, None])[:, :, None], lanes)
    cdec = jnp.exp(ts * log_g)[:, None, None]
    return cos2 * qs, sin2 * qs, cos2, sin2, mask, xi, zeta, cdec


def _mixer_ln(x, w_in, b_in, ret_gn_g, conv_k, conv_b, conv_ln_g, conv_ln_b,
              w_ret_o, w_conv_o, w_out, ln_g, ln_b, *, alpha):
    bsz, seq, d = x.shape
    ts = min(MIX_ROW_TILE, seq)
    assert seq % ts == 0 and ts % CONV_ROW_BLOCK == 0 and d % V7X_LANES == 0
    in_w = w_in.shape[1]
    v_w = RET_HEADS * RET_DV
    qk_w = RET_HEADS * RET_DK
    n_cblk = d // V7X_LANES
    cosq, sinq, cosk, sink, mask, xi, zeta, cdec = _retention_tables(seq, ts)
    conv_k3 = conv_k.reshape(CONV_WIDTH, n_cblk, V7X_LANES).transpose(1, 0, 2)
    kern = functools.partial(_mixer_ln_kernel, alpha=alpha, d_model=d)
    pos_spec = pl.BlockSpec((ts, RET_DK), lambda b, i: (i, 0))
    return pl.pallas_call(
        kern,
        grid=(bsz, seq // ts),
        in_specs=[
            pl.BlockSpec((1, ts, d), lambda b, i: (b, i, 0)),
            pl.BlockSpec(memory_space=pl.ANY),
            _resident((1, in_w)),
            pos_spec, pos_spec, pos_spec, pos_spec,
            _resident((RET_HEADS, ts, ts)),
            _resident((RET_HEADS, ts, RET_DK)),
            _resident((RET_HEADS, ts, RET_DK)),
            _resident((RET_HEADS, 1, 1)),
            _resident((1, v_w)),
            _resident((n_cblk, CONV_WIDTH, V7X_LANES)),
            _resident((1, d)),
            _resident((1, d)),
            _resident((1, d)),
            pl.BlockSpec(memory_space=pl.ANY),
            pl.BlockSpec(memory_space=pl.ANY),
            pl.BlockSpec(memory_space=pl.ANY),
            _resident((1, d)),
            _resident((1, d)),
        ],
        out_specs=pl.BlockSpec((1, ts, d), lambda b, i: (b, i, 0)),
        out_shape=jax.ShapeDtypeStruct((bsz, seq, d), jnp.float32),
        scratch_shapes=[
            pltpu.VMEM((RET_HEADS, RET_DK, RET_DV), jnp.float32),
            pltpu.VMEM((n_cblk, CONV_HALO + ts, V7X_LANES), jnp.float32),
            pltpu.VMEM((n_cblk, ts, V7X_LANES), jnp.float32),
            pltpu.VMEM((ts, v_w), jnp.bfloat16),
            pltpu.VMEM((ts, qk_w), jnp.bfloat16),
            pltpu.VMEM((ts, qk_w), jnp.bfloat16),
            pltpu.VMEM((ts, qk_w), jnp.bfloat16),
            pltpu.VMEM((ts, qk_w), jnp.bfloat16),
            pltpu.VMEM((ts, v_w), jnp.bfloat16),
            pltpu.VMEM((d, in_w), jnp.bfloat16),
            pltpu.VMEM((v_w, d), jnp.bfloat16),
            pltpu.VMEM((d, d), jnp.bfloat16),
            pltpu.VMEM((d, d), jnp.bfloat16),
            pltpu.SemaphoreType.DMA((4,)),
        ],
        compiler_params=pltpu.CompilerParams(
            dimension_semantics=("arbitrary", "arbitrary"), vmem_limit_bytes=VMEM_LIMIT_BYTES),
        name="mixer_ln",
    )(x, _bf16(w_in), b_in.reshape(1, in_w), cosq, sinq, cosk, sink,
      mask, xi, zeta, cdec, ret_gn_g.reshape(1, v_w), conv_k3, conv_b.reshape(1, d),
      conv_ln_g.reshape(1, d), conv_ln_b.reshape(1, d), _bf16(w_ret_o),
      _bf16(w_conv_o), _bf16(w_out), ln_g.reshape(1, d), ln_b.reshape(1, d))


def kernel(x, ffn1_w_gate, ffn1_w_up, ffn1_w_down, ln1_g, ln1_b, w_in, b_in, ret_gn_g, conv_k, conv_b,
           conv_ln_g, conv_ln_b, w_ret_o, w_conv_o, w_out, ln2_g, ln2_b, ffn2_w_gate, ffn2_w_up,
           ffn2_w_down, ln3_g, ln3_b):
    depth = ffn1_w_gate.shape[0]
    alpha = (2.0 * depth) ** 0.25
    bsz, seq, d = x.shape
    for l in range(depth):
        x = _ffn_ln(x.reshape(bsz * seq, d), ffn1_w_gate[l], ffn1_w_up[l], ffn1_w_down[l],
                    ln1_g[l], ln1_b[l], alpha=alpha).reshape(bsz, seq, d)
        x = _mixer_ln(x, w_in[l], b_in[l], ret_gn_g[l], conv_k[l], conv_b[l], conv_ln_g[l], conv_ln_b[l],
                      w_ret_o[l], w_conv_o[l], w_out[l], ln2_g[l], ln2_b[l], alpha=alpha)
        x = _ffn_ln(x.reshape(bsz * seq, d), ffn2_w_gate[l], ffn2_w_up[l], ffn2_w_down[l],
                    ln3_g[l], ln3_b[l], alpha=alpha).reshape(bsz, seq, d)
    return x
```

```python
import functools
import math

import jax
import jax.numpy as jnp
import numpy as np
from jax import lax
from jax.experimental import pallas as pl
from jax.experimental.pallas import tpu as pltpu

RET_HEADS = 8
RET_DK = 128
RET_DV = 256
CONV_WIDTH = 31
ROPE_BASE = 10000.0
LN_EPS = 1e-5
LOG2_E = math.log2(math.e)

V7X_LANES = 128
V7X_VMEM_BYTES = 64 * 1024 * 1024
VMEM_LIMIT_BYTES = V7X_VMEM_BYTES - 8 * 1024 * 1024

FFN_ROW_TILE = 1024
FFN_ROW_SUB = 256
MIX_ROW_TILE = 256
CONV_HALO = 32
CONV_ROW_BLOCK = 64


def _resident(shape):
    nd = len(shape)
    return pl.BlockSpec(shape, lambda *_: (0,) * nd, pipeline_mode=pl.Buffered(1))


def _layer_norm(y, g, b):
    mu = jnp.mean(y, axis=-1, keepdims=True)
    d = y - mu
    var = jnp.mean(d * d, axis=-1, keepdims=True)
    return d * lax.rsqrt(var + LN_EPS) * g + b


def _sigmoid(x):
    return 1.0 / (1.0 + jnp.exp2(x * (-LOG2_E)))


def _silu(x):
    return x * _sigmoid(x)


def _dot(a, b):
    return jnp.dot(a, b, preferred_element_type=jnp.float32)


def _bf16(w):
    return w.astype(jnp.bfloat16)


def _fetch_weights(pairs, sem):
    copies = [pltpu.make_async_copy(w_hbm, w_vmem, sem.at[i]) for i, (w_hbm, w_vmem) in enumerate(pairs)]
    for cp in copies:
        cp.start()
    for cp in copies:
        cp.wait()


def _ffn_ln_kernel(x_ref, wg_hbm, wu_hbm, wd_hbm, g_ref, b_ref, pg_ref, pb_ref, o_ref,
                   h_ref, wg_ref, wu_ref, wd_ref, sem, *, alpha, row_sub, pre_ln):
    @pl.when(pl.program_id(0) == 0)
    def _():
        _fetch_weights(((wg_hbm, wg_ref), (wu_hbm, wu_ref), (wd_hbm, wd_ref)), sem)

    for r in range(x_ref.shape[0] // row_sub):
        rows = slice(r * row_sub, (r + 1) * row_sub)
        x = x_ref[rows, :]
        if pre_ln:
            x = _layer_norm(x, pg_ref[...], pb_ref[...])
        xb = x.astype(jnp.bfloat16)
        h_ref[rows, :] = (_silu(_dot(xb, wg_ref[...])) * _dot(xb, wu_ref[...])).astype(jnp.bfloat16)
        y = alpha * x + 0.5 * _dot(h_ref[rows, :], wd_ref[...])
        o_ref[rows, :] = _layer_norm(y, g_ref[...], b_ref[...])


def _ffn_ln(x2d, w_gate, w_up, w_down, ln_g, ln_b, *, alpha, pre=None):
    t, d = x2d.shape
    d_ff = w_gate.shape[1]
    tm = min(FFN_ROW_TILE, t)
    assert t % tm == 0 and tm % FFN_ROW_SUB == 0
    kern = functools.partial(_ffn_ln_kernel, alpha=alpha, row_sub=FFN_ROW_SUB, pre_ln=pre is not None)
    pre_g, pre_b = pre if pre is not None else (ln_g, ln_b)
    return pl.pallas_call(
        kern,
        grid=(t // tm,),
        in_specs=[
            pl.BlockSpec((tm, d), lambda i: (i, 0)),
            pl.BlockSpec(memory_space=pl.ANY),
            pl.BlockSpec(memory_space=pl.ANY),
            pl.BlockSpec(memory_space=pl.ANY),
            _resident((1, d)),
            _resident((1, d)),
            _resident((1, d)),
            _resident((1, d)),
        ],
        out_specs=pl.BlockSpec((tm, d), lambda i: (i, 0)),
        out_shape=jax.ShapeDtypeStruct((t, d), jnp.float32),
        scratch_shapes=[
            pltpu.VMEM((tm, d_ff), jnp.bfloat16),
            pltpu.VMEM((d, d_ff), jnp.bfloat16),
            pltpu.VMEM((d, d_ff), jnp.bfloat16),
            pltpu.VMEM((d_ff, d), jnp.bfloat16),
            pltpu.SemaphoreType.DMA((3,)),
        ],
        compiler_params=pltpu.CompilerParams(
            dimension_semantics=("arbitrary",), vmem_limit_bytes=VMEM_LIMIT_BYTES),
        name="ffn_ln",
    )(x2d, _bf16(w_gate), _bf16(w_up), _bf16(w_down),
      ln_g.reshape(1, d), ln_b.reshape(1, d), pre_g.reshape(1, d), pre_b.reshape(1, d))


def _conv_channel_block(ubuf_ref, ybuf_ref, conv_k_ref, cb, ts):
    kcb = conv_k_ref[cb]
    for rb in range(ts // CONV_ROW_BLOCK):
        base = rb * CONV_ROW_BLOCK + CONV_HALO - (CONV_WIDTH - 1)
        acc = ubuf_ref[cb, base:base + CONV_ROW_BLOCK, :] * kcb[0:1, :]
        for j in range(1, CONV_WIDTH):
            acc = acc + ubuf_ref[cb, base + j:base + j + CONV_ROW_BLOCK, :] * kcb[j:j + 1, :]
        ybuf_ref[cb, rb * CONV_ROW_BLOCK:(rb + 1) * CONV_ROW_BLOCK, :] = acc
    ubuf_ref[cb, 0:CONV_HALO, :] = ubuf_ref[cb, ts:ts + CONV_HALO, :]


def _mixer_ln_kernel(x_ref, w_in_hbm, b_in_ref, cosq_ref, sinq_ref, cosk_ref, sink_ref,
                     mask_ref, xi_ref, zeta_ref, cdec_ref, gn_g_ref, conv_k_ref, conv_b_ref,
                     cln_g_ref, cln_b_ref, w_ret_o_hbm, w_conv_o_hbm, w_out_hbm, ln_g_ref, ln_b_ref,
                     o_ref,
                     state_ref, ubuf_ref, ybuf_ref, gated_ref, q_ref, qx_ref, k_ref, kz_ref, v_ref,
                     w_in_ref, w_ret_o_ref, w_conv_o_ref, w_out_ref, sem,
                     *, alpha, d_model):
    ts = x_ref.shape[1]
    qk_w = RET_HEADS * RET_DK
    v_w = RET_HEADS * RET_DV
    n_cblk = d_model // V7X_LANES
    off_k = qk_w
    off_v = 2 * qk_w
    off_g = off_v + v_w
    off_a = off_g + v_w
    off_b = off_a + d_model
    off_gr = off_b + d_model
    off_gc = off_gr + d_model

    @pl.when((pl.program_id(0) == 0) & (pl.program_id(1) == 0))
    def _():
        _fetch_weights(((w_in_hbm, w_in_ref), (w_ret_o_hbm, w_ret_o_ref),
                        (w_conv_o_hbm, w_conv_o_ref), (w_out_hbm, w_out_ref)), sem)

    @pl.when(pl.program_id(1) == 0)
    def _():
        state_ref[...] = jnp.zeros_like(state_ref)
        ubuf_ref[:, 0:CONV_HALO, :] = jnp.zeros((n_cblk, CONV_HALO, V7X_LANES), jnp.float32)

    x = x_ref[0]
    xb = x.astype(jnp.bfloat16)

    def proj(off, width):
        return _dot(xb, w_in_ref[:, off:off + width]) + b_in_ref[:, off:off + width]

    u = proj(off_a, d_model) * _sigmoid(proj(off_b, d_model))
    for cb in range(n_cblk):
        ubuf_ref[cb, CONV_HALO:CONV_HALO + ts, :] = u[:, cb * V7X_LANES:(cb + 1) * V7X_LANES]

    cblocks = iter(range(n_cblk))

    def conv_some(n):
        for _ in range(n):
            cb = next(cblocks, None)
            if cb is not None:
                _conv_channel_block(ubuf_ref, ybuf_ref, conv_k_ref, cb, ts)

    per_stage = -(-n_cblk // 4)
    qa = proj(0, qk_w)
    conv_some(per_stage)
    cosq, sinq = cosq_ref[...], sinq_ref[...]
    for h in range(RET_HEADS):
        sl = slice(h * RET_DK, (h + 1) * RET_DK)
        qh = qa[:, sl]
        qh = qh * cosq + pltpu.roll(qh, RET_DK // 2, 1) * sinq
        q_ref[:, sl] = qh.astype(jnp.bfloat16)
        qx_ref[:, sl] = (qh * xi_ref[h]).astype(jnp.bfloat16)
    ka = proj(off_k, qk_w)
    conv_some(per_stage)
    cosk, sink = cosk_ref[...], sink_ref[...]
    for h in range(RET_HEADS):
        sl = slice(h * RET_DK, (h + 1) * RET_DK)
        kh = ka[:, sl]
        kh = kh * cosk + pltpu.roll(kh, RET_DK // 2, 1) * sink
        k_ref[:, sl] = kh.astype(jnp.bfloat16)
        kz_ref[:, sl] = (kh * zeta_ref[h]).astype(jnp.bfloat16)
    v_ref[:, 0:v_w // 2] = proj(off_v, v_w // 2).astype(jnp.bfloat16)
    conv_some(per_stage)
    v_ref[:, v_w // 2:v_w] = proj(off_v + v_w // 2, v_w // 2).astype(jnp.bfloat16)
    conv_some(n_cblk)

    for h in range(RET_HEADS):
        sk = slice(h * RET_DK, (h + 1) * RET_DK)
        sv = slice(h * RET_DV, (h + 1) * RET_DV)
        gh = proj(off_g + h * RET_DV, RET_DV)
        vh = v_ref[:, sv]
        scores = lax.dot_general(q_ref[:, sk], k_ref[:, sk], (((1,), (1,)), ((), ())),
                                 preferred_element_type=jnp.float32) * mask_ref[h]
        state = state_ref[h]
        lhs = jnp.concatenate([scores.astype(jnp.bfloat16), qx_ref[:, sk]], axis=1)
        rhs = jnp.concatenate([vh, state.astype(jnp.bfloat16)], axis=0)
        r = _dot(lhs, rhs)
        state_ref[h] = cdec_ref[h] * state + lax.dot_general(
            kz_ref[:, sk], vh, (((0,), (0,)), ((), ())), preferred_element_type=jnp.float32)
        mu = jnp.mean(r, axis=-1, keepdims=True)
        d = r - mu
        var = jnp.mean(d * d, axis=-1, keepdims=True)
        rn = d * lax.rsqrt(var + LN_EPS) * gn_g_ref[:, sv]
        gated_ref[:, sv] = (_silu(gh) * rn).astype(jnp.bfloat16)

    gate_r = _sigmoid(proj(off_gr, d_model))
    y = jnp.concatenate([ybuf_ref[cb] for cb in range(n_cblk)], axis=1) + conv_b_ref[...]
    cu = _silu(_layer_norm(y, cln_g_ref[...], cln_b_ref[...]))
    ret_out = _dot(gated_ref[...], w_ret_o_ref[...])
    gate_c = _sigmoid(proj(off_gc, d_model))
    conv_out = _dot(cu.astype(jnp.bfloat16), w_conv_o_ref[...])
    merged = gate_r * ret_out + gate_c * conv_out
    m = _dot(merged.astype(jnp.bfloat16), w_out_ref[...])
    o_ref[0] = alpha * x + m


def _retention_tables(seq, ts):
    half = RET_DK // 2
    freqs = ROPE_BASE ** (-jnp.arange(half, dtype=jnp.float32) / half)
    ang = jnp.arange(seq, dtype=jnp.float32)[:, None] * freqs[None, :]
    cos, sin = jnp.cos(ang), jnp.sin(ang)
    cos2 = jnp.concatenate([cos, cos], axis=1)
    sin2 = jnp.concatenate([-sin, sin], axis=1)
    qs = RET_DK ** -0.5
    log_g = jnp.log(1.0 - jnp.exp2(-5.0 - jnp.arange(RET_HEADS, dtype=jnp.float32)))
    idx = jnp.arange(ts, dtype=jnp.float32)
    diff = idx[:, None] - idx[None, :]
    mask = jnp.where(diff[None] >= 0, jnp.exp(jnp.maximum(diff, 0.0)[None] * log_g[:, None, None]), 0.0)
    lanes = (RET_HEADS, ts, RET_DK)
    xi = jnp.broadcast_to(jnp.exp((idx[None, :] + 1.0) * log_g[:, None])[:, :, None], lanes)
    zeta = jnp.broadcast_to(jnp.exp((ts - 1.0 - idx)[None, :] * log_g[:, None])[:, :, None], lanes)
    cdec = jnp.exp(ts * log_g)[:, None, None]
    return cos2 * qs, sin2 * qs, cos2, sin2, mask, xi, zeta, cdec


def _mixer_ln(x, w_in, b_in, ret_gn_g, conv_k, conv_b, conv_ln_g, conv_ln_b,
              w_ret_o, w_conv_o, w_out, ln_g, ln_b, *, alpha):
    bsz, seq, d = x.shape
    ts = min(MIX_ROW_TILE, seq)
    assert seq % ts == 0 and ts % CONV_ROW_BLOCK == 0 and d % V7X_LANES == 0
    in_w = w_in.shape[1]
    v_w = RET_HEADS * RET_DV
    qk_w = RET_HEADS * RET_DK
    n_cblk = d // V7X_LANES
    cosq, sinq, cosk, sink, mask, xi, zeta, cdec = _retention_tables(seq, ts)
    conv_k3 = conv_k.reshape(CONV_WIDTH, n_cblk, V7X_LANES).transpose(1, 0, 2)
    kern = functools.partial(_mixer_ln_kernel, alpha=alpha, d_model=d)
    pos_spec = pl.BlockSpec((ts, RET_DK), lambda b, i: (i, 0))
    return pl.pallas_call(
        kern,
        grid=(bsz, seq // ts),
        in_specs=[
            pl.BlockSpec((1, ts, d), lambda b, i: (b, i, 0)),
            pl.BlockSpec(memory_space=pl.ANY),
            _resident((1, in_w)),
            pos_spec, pos_spec, pos_spec, pos_spec,
            _resident((RET_HEADS, ts, ts)),
            _resident((RET_HEADS, ts, RET_DK)),
            _resident((RET_HEADS, ts, RET_DK)),
            _resident((RET_HEADS, 1, 1)),
            _resident((1, v_w)),
            _resident((n_cblk, CONV_WIDTH, V7X_LANES)),
            _resident((1, d)),
            _resident((1, d)),
            _resident((1, d)),
            pl.BlockSpec(memory_space=pl.ANY),
            pl.BlockSpec(memory_space=pl.ANY),
            pl.BlockSpec(memory_space=pl.ANY),
            _resident((1, d)),
            _resident((1, d)),
        ],
        out_specs=pl.BlockSpec((1, ts, d), lambda b, i: (b, i, 0)),
        out_shape=jax.ShapeDtypeStruct((bsz, seq, d), jnp.float32),
        scratch_shapes=[
            pltpu.VMEM((RET_HEADS, RET_DK, RET_DV), jnp.float32),
            pltpu.VMEM((n_cblk, CONV_HALO + ts, V7X_LANES), jnp.float32),
            pltpu.VMEM((n_cblk, ts, V7X_LANES), jnp.float32),
            pltpu.VMEM((ts, v_w), jnp.bfloat16),
            pltpu.VMEM((ts, qk_w), jnp.bfloat16),
            pltpu.VMEM((ts, qk_w), jnp.bfloat16),
            pltpu.VMEM((ts, qk_w), jnp.bfloat16),
            pltpu.VMEM((ts, qk_w), jnp.bfloat16),
            pltpu.VMEM((ts, v_w), jnp.bfloat16),
            pltpu.VMEM((d, in_w), jnp.bfloat16),
            pltpu.VMEM((v_w, d), jnp.bfloat16),
            pltpu.VMEM((d, d), jnp.bfloat16),
            pltpu.VMEM((d, d), jnp.bfloat16),
            pltpu.SemaphoreType.DMA((4,)),
        ],
        compiler_params=pltpu.CompilerParams(
            dimension_semantics=("arbitrary", "arbitrary"), vmem_limit_bytes=VMEM_LIMIT_BYTES),
        name="mixer_ln",
    )(x, _bf16(w_in), b_in.reshape(1, in_w), cosq, sinq, cosk, sink,
      mask, xi, zeta, cdec, ret_gn_g.reshape(1, v_w), conv_k3, conv_b.reshape(1, d),
      conv_ln_g.reshape(1, d), conv_ln_b.reshape(1, d), _bf16(w_ret_o),
      _bf16(w_conv_o), _bf16(w_out), ln_g.reshape(1, d), ln_b.reshape(1, d))


def kernel(x, ffn1_w_gate, ffn1_w_up, ffn1_w_down, ln1_g, ln1_b, w_in, b_in, ret_gn_g, conv_k, conv_b,
           conv_ln_g, conv_ln_b, w_ret_o, w_conv_o, w_out, ln2_g, ln2_b, ffn2_w_gate, ffn2_w_up,
           ffn2_w_down, ln3_g, ln3_b):
    depth = ffn1_w_gate.shape[0]
    alpha = (2.0 * depth) ** 0.25
    bsz, seq, d = x.shape
    for l in range(depth):
        x = _ffn_ln(x.reshape(bsz * seq, d), ffn1_w_gate[l], ffn1_w_up[l], ffn1_w_down[l],
                    ln1_g[l], ln1_b[l], alpha=alpha).reshape(bsz, seq, d)
        x = _mixer_ln(x, w_in[l], b_in[l], ret_gn_g[l], conv_k[l], conv_b[l], conv_ln_g[l], conv_ln_b[l],
                      w_ret_o[l], w_conv_o[l], w_out[l], ln2_g[l], ln2_b[l], alpha=alpha)
        x = _ffn_ln(x.reshape(bsz * seq, d), ffn2_w_gate[l], ffn2_w_up[l], ffn2_w_down[l],
                    ln3_g[l], ln3_b[l], alpha=alpha, pre=(ln2_g[l], ln2_b[l])).reshape(bsz, seq, d)
    return x
```
